```python
import jax, jax.numpy as jnp
from jax import lax
import numpy as np

D_MODEL = 2048
BATCH = 4
SEQ = 2048
DEPTH = 4
DEC_BATCH = 32
DEC_SEQ = 4
PAST_LEN = 16384
PAGE_SIZE = 128

N_MIXERS = 3
N_CONV = (DEPTH + 2) // 3
N_SWA = (DEPTH + 1) // 3
N_FOX = DEPTH // 3
CONV_W = 31
D_CONV = D_MODEL
SWA_HD = 64
SWA_H = D_MODEL // SWA_HD
SWA_KV = SWA_H // 8
SWA_G = SWA_H // SWA_KV
WINDOW = 128
FOX_HD = 128
FOX_H = D_MODEL // FOX_HD
FOX_KV = FOX_H // 4
FOX_G = FOX_H // FOX_KV
Q_BLOCK = 128
D_FF = ((8 * D_MODEL + 3 * 256 - 1) // (3 * 256)) * 256
D_PLE = 256
ROPE_THETA = 10000.0
FORGET_BIAS = 8.0
EPS = 1e-6
NEG = -1e30

kernel_name = "hybrid_conv_swa_fox_decoder_step"


def rmsnorm(x, g):
    xf = x.astype(jnp.float32)
    y = xf * lax.rsqrt(jnp.mean(xf * xf, axis=-1, keepdims=True) + EPS)
    return (y * g.astype(jnp.float32)).astype(x.dtype)


def layernorm(x, g, b):
    xf = x.astype(jnp.float32)
    mu = jnp.mean(xf, axis=-1, keepdims=True)
    var = jnp.mean(jnp.square(xf - mu), axis=-1, keepdims=True)
    y = (xf - mu) * lax.rsqrt(var + EPS) * g.astype(jnp.float32) + b.astype(jnp.float32)
    return y.astype(x.dtype)


def rope(x, pos):
    hd = x.shape[-1]
    half = hd // 2
    inv = ROPE_THETA ** (-2.0 * jnp.arange(half, dtype=jnp.float32) / hd)
    ang = pos.astype(jnp.float32)[:, None] * inv[None, :]
    cos = jnp.cos(ang)[None, :, None, :]
    sin = jnp.sin(ang)[None, :, None, :]
    xf = x.astype(jnp.float32)
    x1, x2 = xf[..., :half], xf[..., half:]
    return jnp.concatenate([x1 * cos - x2 * sin, x2 * cos + x1 * sin], axis=-1).astype(x.dtype)


def conv_mixer(h, buf, w_in, w_dw, b_dw, ln_g, ln_b, w_out):
    ab = h @ w_in
    u = ab[..., :D_CONV] * jax.nn.sigmoid(ab[..., D_CONV:])
    ext = jnp.concatenate([buf.astype(u.dtype), u], axis=1)
    y = lax.conv_general_dilated(ext, w_dw[:, None, :].astype(ext.dtype), (1,), "VALID",
                                 dimension_numbers=("NWC", "WIO", "NWC"),
                                 feature_group_count=D_CONV) + b_dw
    y = jax.nn.silu(layernorm(y, ln_g, ln_b))
    return y @ w_out, ext[:, -(CONV_W - 1):]


def swa_qkv(h, pos, w_qkv):
    N, S, _ = h.shape
    z = h @ w_qkv
    nq, nk = SWA_H * SWA_HD, SWA_KV * SWA_HD
    q = rope(z[..., :nq].reshape(N, S, SWA_H, SWA_HD), pos).reshape(N, S, SWA_KV, SWA_G, SWA_HD)
    k = rope(z[..., nq:nq + nk].reshape(N, S, SWA_KV, SWA_HD), pos)
    v = z[..., nq + nk:].reshape(N, S, SWA_KV, SWA_HD)
    return q, k, v


def swa_core(q, k, v, sinks, qpos, kpos):
    s = jnp.einsum("...qkgd,...skd->...kgqs", q, k, preferred_element_type=jnp.float32) * (SWA_HD ** -0.5)
    d = qpos[..., :, None] - kpos[..., None, :]
    valid = (d >= 0) & (d <= WINDOW) & (kpos[..., None, :] >= 0)
    s = jnp.where(valid[..., None, None, :, :], s, NEG)
    sink = jnp.broadcast_to(sinks.astype(jnp.float32).reshape(SWA_KV, SWA_G, 1, 1), s.shape[:-1] + (1,))
    p = jax.nn.softmax(jnp.concatenate([s, sink], axis=-1), axis=-1)[..., :-1]
    return jnp.einsum("...kgqs,...skd->...qkgd", p.astype(v.dtype), v)


def swa_prompt(h, pos, w_qkv, sinks, w_o):
    N, S, _ = h.shape
    q, k, v = swa_qkv(h, pos, w_qkv)
    nb = S // Q_BLOCK
    qb = q.reshape(N, nb, Q_BLOCK, SWA_KV, SWA_G, SWA_HD)
    kb = k.reshape(N, nb, Q_BLOCK, SWA_KV, SWA_HD)
    vb = v.reshape(N, nb, Q_BLOCK, SWA_KV, SWA_HD)
    padb = ((0, 0), (1, 0), (0, 0), (0, 0), (0, 0))
    kcat = jnp.concatenate([jnp.pad(kb[:, :-1], padb), kb], axis=2)
    vcat = jnp.concatenate([jnp.pad(vb[:, :-1], padb), vb], axis=2)
    starts = jnp.arange(nb, dtype=jnp.int32) * Q_BLOCK
    qpos = starts[:, None] + jnp.arange(Q_BLOCK, dtype=jnp.int32)[None, :]
    kpos = starts[:, None] - Q_BLOCK + jnp.arange(2 * Q_BLOCK, dtype=jnp.int32)[None, :]
    o = swa_core(qb, kcat, vcat, sinks, qpos, kpos).reshape(N, S, SWA_H * SWA_HD)
    return o @ w_o, k[:, -WINDOW:], v[:, -WINDOW:]


def swa_sample(h, pos, kbuf, vbuf, w_qkv, sinks, w_o):
    N, T, _ = h.shape
    q, k, v = swa_qkv(h, pos, w_qkv)
    nbuf = kbuf.shape[1]
    kall = jnp.concatenate([kbuf.astype(k.dtype), k], axis=1)
    vall = jnp.concatenate([vbuf.astype(v.dtype), v], axis=1)
    kpos = jnp.concatenate([pos[0] - nbuf + jnp.arange(nbuf, dtype=jnp.int32), pos])
    o = swa_core(q, kall, vall, sinks, pos, kpos).reshape(N, T, SWA_H * SWA_HD)
    return o @ w_o, kall[:, -WINDOW:], vall[:, -WINDOW:]


def fox_proj(h, w_in, b_f):
    N, S, _ = h.shape
    nq, nk = FOX_H * FOX_HD, FOX_KV * FOX_HD
    z = h @ w_in
    q = z[..., :nq].reshape(N, S, FOX_KV, FOX_G, FOX_HD)
    k = z[..., nq:nq + nk].reshape(N, S, FOX_KV, FOX_HD)
    v = z[..., nq + nk:nq + 2 * nk].reshape(N, S, FOX_KV, FOX_HD)
    logf = jax.nn.log_sigmoid(z[..., nq + 2 * nk:].astype(jnp.float32) + b_f.astype(jnp.float32))
    return q, k, v, logf


def fox_core(q, k, v, cq, ck, valid):
    N, Tq = q.shape[:2]
    Ts = k.shape[1]
    s = jnp.einsum("nqkgd,nskd->nkgqs", q, k, preferred_element_type=jnp.float32) * (FOX_HD ** -0.5)
    bq = cq.reshape(N, Tq, FOX_KV, FOX_G).transpose(0, 2, 3, 1)[..., :, None]
    bk = ck.reshape(N, Ts, FOX_KV, FOX_G).transpose(0, 2, 3, 1)[..., None, :]
    s = jnp.where(valid, s + (bq - bk), NEG)
    p = jax.nn.softmax(s, axis=-1)
    return jnp.einsum("nkgqs,nskd->nqkgd", p.astype(v.dtype), v)


def fox_prompt(h, w_in, b_f, w_o):
    N, S, _ = h.shape
    q, k, v, logf = fox_proj(h, w_in, b_f)
    c = lax.cumsum(logf, axis=1)
    nb = S // Q_BLOCK
    qb = q.reshape(N, nb, Q_BLOCK, FOX_KV, FOX_G, FOX_HD).transpose(1, 0, 2, 3, 4, 5)
    cb = c.reshape(N, nb, Q_BLOCK, FOX_H).transpose(1, 0, 2, 3)
    starts = jnp.arange(nb, dtype=jnp.int32) * Q_BLOCK
    kpos = jnp.arange(S, dtype=jnp.int32)

    def block(args):
        qi, ci, s0 = args
        qpos = s0 + jnp.arange(Q_BLOCK, dtype=jnp.int32)
        return fox_core(qi, k, v, ci, c, qpos[:, None] >= kpos[None, :])

    o = lax.map(block, (qb, cb, starts))
    o = o.transpose(1, 0, 2, 3, 4, 5).reshape(N, S, FOX_H * FOX_HD)
    return o @ w_o, k, v, logf


def fox_sample(h, cache_k, cache_v, cache_logf, page_table, w_in, b_f, w_o):
    N, T, _ = h.shape
    q, k, v, logf = fox_proj(h, w_in, b_f)
    past = page_table.shape[1] * cache_k.shape[1]
    kp = cache_k[page_table].reshape(N, past, FOX_KV, FOX_HD)
    vp = cache_v[page_table].reshape(N, past, FOX_KV, FOX_HD)
    lp = cache_logf[page_table].reshape(N, past, FOX_H).astype(jnp.float32)
    g_past = lp - lax.cumsum(lp, axis=1, reverse=True)
    cn = lax.cumsum(logf, axis=1)
    ck = jnp.concatenate([g_past, cn], axis=1)
    kall = jnp.concatenate([kp.astype(k.dtype), k], axis=1)
    vall = jnp.concatenate([vp.astype(v.dtype), v], axis=1)
    valid = jnp.concatenate([jnp.ones((T, past), bool), jnp.tril(jnp.ones((T, T), bool))], axis=1)
    o = fox_core(q, kall, vall, cn, ck, valid).reshape(N, T, FOX_H * FOX_HD)
    return o @ w_o, k, v, logf


def ffn_and_ple(x, p, g_ffn, w_gate, w_up, w_down, g_ple, w_proj, w_pgate):
    h = rmsnorm(x, g_ffn)
    x = x + (jax.nn.silu(h @ w_gate) * (h @ w_up)) @ w_down
    gate = jax.nn.sigmoid(rmsnorm(x, g_ple) @ w_pgate)
    return x + (p.astype(x.dtype) @ w_proj) * gate


def setup_inputs(seed: int = 0) -> dict:
    key = jax.random.key(seed)
    ks = iter(jax.random.split(key, 64))
    f32 = jnp.float32

    def nrm(shape, scale):
        return jax.random.normal(next(ks), shape, f32) * scale

    def gain(shape):
        return 1.0 + nrm(shape, 0.05)

    n_pages = PAST_LEN // PAGE_SIZE
    n_used = DEC_BATCH * n_pages
    n_pool = n_used + (n_used + 3) // 4
    out_s = (2 * DEPTH) ** -0.5
    page_table = jax.random.permutation(next(ks), n_pool)[:n_used].reshape(DEC_BATCH, n_pages).astype(jnp.int32)
    fox_cols = FOX_H * FOX_HD + 2 * FOX_KV * FOX_HD + FOX_H
    return {
        "x_prompt": nrm((BATCH, SEQ, D_MODEL), 1.0),
        "x_sample": nrm((DEC_BATCH, DEC_SEQ, D_MODEL), 1.0),
        "p_prompt": nrm((DEPTH, BATCH, SEQ, D_PLE), 1.0),
        "p_sample": nrm((DEPTH, DEC_BATCH, DEC_SEQ, D_PLE), 1.0),
        "cache_conv": nrm((N_CONV, DEC_BATCH, CONV_W - 1, D_CONV), 0.5),
        "cache_swa_k": nrm((N_SWA, DEC_BATCH, WINDOW, SWA_KV, SWA_HD), 1.0),
        "cache_swa_v": nrm((N_SWA, DEC_BATCH, WINDOW, SWA_KV, SWA_HD), 1.0),
        "cache_fox_k": nrm((N_FOX, n_pool, PAGE_SIZE, FOX_KV, FOX_HD), 1.0),
        "cache_fox_v": nrm((N_FOX, n_pool, PAGE_SIZE, FOX_KV, FOX_HD), 1.0),
        "cache_fox_logf": jax.nn.log_sigmoid(FORGET_BIAS + nrm((N_FOX, n_pool, PAGE_SIZE, FOX_H), 0.5)),
        "page_table": page_table,
        "norm_mix": gain((DEPTH, D_MODEL)),
        "norm_ffn": gain((DEPTH, D_MODEL)),
        "norm_ple": gain((DEPTH, D_MODEL)),
        "norm_final": gain((D_MODEL,)),
        "conv_w_in": nrm((N_CONV, D_MODEL, 2 * D_CONV), D_MODEL ** -0.5),
        "conv_w_dw": nrm((N_CONV, CONV_W, D_CONV), CONV_W ** -0.5),
        "conv_b_dw": nrm((N_CONV, D_CONV), 0.02),
        "conv_ln_g": gain((N_CONV, D_CONV)),
        "conv_ln_b": nrm((N_CONV, D_CONV), 0.02),
        "conv_w_out": nrm((N_CONV, D_CONV, D_MODEL), D_CONV ** -0.5 * out_s),
        "swa_w_qkv": nrm((N_SWA, D_MODEL, (SWA_H + 2 * SWA_KV) * SWA_HD), D_MODEL ** -0.5),
        "swa_sinks": nrm((N_SWA, SWA_H), 0.5),
        "swa_w_o": nrm((N_SWA, SWA_H * SWA_HD, D_MODEL), (SWA_H * SWA_HD) ** -0.5 * out_s),
        "fox_w_in": nrm((N_FOX, D_MODEL, fox_cols), D_MODEL ** -0.5),
        "fox_b_f": FORGET_BIAS + nrm((N_FOX, FOX_H), 0.5),
        "fox_w_o": nrm((N_FOX, FOX_H * FOX_HD, D_MODEL), (FOX_H * FOX_HD) ** -0.5 * out_s),
        "ffn_w_gate": nrm((DEPTH, D_MODEL, D_FF), D_MODEL ** -0.5),
        "ffn_w_up": nrm((DEPTH, D_MODEL, D_FF), D_MODEL ** -0.5),
        "ffn_w_down": nrm((DEPTH, D_FF, D_MODEL), D_FF ** -0.5 * out_s),
        "ple_w_proj": nrm((DEPTH, D_PLE, D_MODEL), D_PLE ** -0.5 * out_s),
        "ple_w_gate": nrm((DEPTH, D_MODEL, D_MODEL), D_MODEL ** -0.5),
    }


def reference(x_prompt, x_sample, p_prompt, p_sample,
              cache_conv, cache_swa_k, cache_swa_v, cache_fox_k, cache_fox_v, cache_fox_logf, page_table,
              norm_mix, norm_ffn, norm_ple, norm_final,
              conv_w_in, conv_w_dw, conv_b_dw, conv_ln_g, conv_ln_b, conv_w_out,
              swa_w_qkv, swa_sinks, swa_w_o,
              fox_w_in, fox_b_f, fox_w_o,
              ffn_w_gate, ffn_w_up, ffn_w_down,
              ple_w_proj, ple_w_gate):
    xp, xs = x_prompt, x_sample
    past_len = page_table.shape[1] * cache_fox_k.shape[2]
    pos_p = jnp.arange(xp.shape[1], dtype=jnp.int32)
    pos_s = past_len + jnp.arange(xs.shape[1], dtype=jnp.int32)
    conv_p, conv_s = [], []
    swk_p, swv_p, swk_s, swv_s = [], [], [], []
    fk_p, fv_p, fl_p, fk_s, fv_s, fl_s = [], [], [], [], [], []

    for i in range(DEPTH):
        kind, j = i % N_MIXERS, i // N_MIXERS
        hp = rmsnorm(xp, norm_mix[i])
        hs = rmsnorm(xs, norm_mix[i])
        if kind == 0:
            w = (conv_w_in[j], conv_w_dw[j], conv_b_dw[j], conv_ln_g[j], conv_ln_b[j], conv_w_out[j])
            zero_buf = jnp.zeros((xp.shape[0], CONV_W - 1, D_CONV), xp.dtype)
            mp, bp = conv_mixer(hp, zero_buf, *w)
            ms, bs = conv_mixer(hs, cache_conv[j], *w)
            conv_p.append(bp)
            conv_s.append(bs)
        elif kind == 1:
            mp, kp_, vp_ = swa_prompt(hp, pos_p, swa_w_qkv[j], swa_sinks[j], swa_w_o[j])
            ms, ks_, vs_ = swa_sample(hs, pos_s, cache_swa_k[j], cache_swa_v[j], swa_w_qkv[j], swa_sinks[j], swa_w_o[j])
            swk_p.append(kp_)
            swv_p.append(vp_)
            swk_s.append(ks_)
            swv_s.append(vs_)
        else:
            mp, kp_, vp_, lp_ = fox_prompt(hp, fox_w_in[j], fox_b_f[j], fox_w_o[j])
            ms, ks_, vs_, ls_ = fox_sample(hs, cache_fox_k[j], cache_fox_v[j], cache_fox_logf[j], page_table,
                                           fox_w_in[j], fox_b_f[j], fox_w_o[j])
            fk_p.append(kp_)
            fv_p.append(vp_)
            fl_p.append(lp_)
            fk_s.append(ks_)
            fv_s.append(vs_)
            fl_s.append(ls_)
        xp = xp + mp
        xs = xs + ms
        wf = (norm_ffn[i], ffn_w_gate[i], ffn_w_up[i], ffn_w_down[i], norm_ple[i], ple_w_proj[i], ple_w_gate[i])
        xp = ffn_and_ple(xp, p_prompt[i], *wf)
        xs = ffn_and_ple(xs, p_sample[i], *wf)

    y_prompt = rmsnorm(xp, norm_final)
    y_sample = rmsnorm(xs, norm_final)
    return (y_prompt, y_sample,
            jnp.stack(conv_p), jnp.stack(conv_s),
            jnp.stack(swk_p), jnp.stack(swv_p), jnp.stack(swk_s), jnp.stack(swv_s),
            jnp.stack(fk_p), jnp.stack(fv_p), jnp.stack(fl_p),
            jnp.stack(fk_s), jnp.stack(fv_s), jnp.stack(fl_s))
```

```python
import functools

import jax
import jax.numpy as jnp
from jax import lax
from jax.experimental import pallas as pl
from jax.experimental.pallas import tpu as pltpu

EPS = 1e-6
NEG = -1e30
ROPE_THETA = 10000.0
F32 = jnp.float32
BF16 = jnp.bfloat16

LANES = 128
BF16_SUBLANES = 16
VMEM_LIMIT = 56 * 1024 * 1024
ROW_TILE_CAP = 640
FFN_COL_TILE_CAP = 512
COL_CHUNK = 512
ATTN_BLOCK = 128
CONV_TIME_TILE_CAP = 128
CONV_ROW_CHUNK = 32
FOX_PAGES_PER_STEP = 8


def _pick_tile(n, cap, mult):
    best = None
    for d in range(mult, min(n, cap) + 1, mult):
        if n % d == 0:
            best = d
    return best if best is not None else n


def _params(*sem):
    return pltpu.CompilerParams(dimension_semantics=sem, vmem_limit_bytes=VMEM_LIMIT)


def _resident(shape):
    nd = len(shape)
    return pl.BlockSpec(shape, lambda *_: (0,) * nd, pipeline_mode=pl.Buffered(1))


def _rmsnorm(x, g):
    return x * lax.rsqrt(jnp.mean(x * x, axis=-1, keepdims=True) + EPS) * g


def _dot(a, b):
    return jnp.dot(a, b, preferred_element_type=F32)


def _dot_nt(a, b):
    return lax.dot_general(a, b, (((1,), (1,)), ((), ())), preferred_element_type=F32)


def _col_chunks(n, chunk=COL_CHUNK):
    return [(s, min(chunk, n - s)) for s in range(0, n, chunk)]


def _split3(x):
    hi = x.astype(BF16)
    r1 = x - hi.astype(F32)
    mid = r1.astype(BF16)
    lo = (r1 - mid.astype(F32)).astype(BF16)
    return hi, mid, lo


def _conv_in_kernel(x_ref, g_ref, w_ref, u_ref):
    h = _rmsnorm(x_ref[...], g_ref[...]).astype(BF16)
    dc = u_ref.shape[-1]
    for s, n in _col_chunks(dc):
        a = _dot(h, w_ref[:, s:s + n])
        b = _dot(h, w_ref[:, dc + s:dc + s + n])
        u_ref[:, s:s + n] = a * jax.nn.sigmoid(b)


def _conv_in(x, g, w, tm):
    m, d = x.shape
    dc = w.shape[1] // 2
    return pl.pallas_call(
        _conv_in_kernel,
        grid=(m // tm,),
        in_specs=[pl.BlockSpec((tm, d), lambda i: (i, 0)), _resident((1, d)), _resident(w.shape)],
        out_specs=pl.BlockSpec((tm, dc), lambda i: (i, 0)),
        out_shape=jax.ShapeDtypeStruct((m, dc), F32),
        compiler_params=_params("parallel"),
        name="conv_in",
    )(x, g, w)


def _swa_in_kernel(x_ref, g_ref, w_ref, cos_ref, sin_ref, q_ref, k_ref, v_ref, *, hd):
    h = _rmsnorm(x_ref[...], g_ref[...]).astype(BF16)
    cos = cos_ref[...]
    sin = sin_ref[...]
    lane = lax.broadcasted_iota(jnp.int32, cos.shape, 1)
    first_half = (lane % hd) < hd // 2

    def rope(z):
        partner = jnp.where(first_half, pltpu.roll(z, LANES - hd // 2, 1), pltpu.roll(z, hd // 2, 1))
        return z * cos + partner * sin

    nq, nk = q_ref.shape[-1], k_ref.shape[-1]
    for s, n in _col_chunks(nq):
        z = _dot(h, w_ref[:, s:s + n])
        for c in range(0, n, LANES):
            q_ref[:, s + c:s + c + LANES] = rope(z[:, c:c + LANES]).astype(q_ref.dtype)
    for s, n in _col_chunks(nk):
        z = _dot(h, w_ref[:, nq + s:nq + s + n])
        for c in range(0, n, LANES):
            k_ref[:, s + c:s + c + LANES] = rope(z[:, c:c + LANES])
    v_ref[...] = _dot(h, w_ref[:, nq + nk:])


def _swa_in(x, g, w, cos, sin, tm, nq, nk, hd):
    m, d = x.shape
    assert LANES % hd == 0 and nq % LANES == 0 and nk % LANES == 0
    row = lambda n: pl.BlockSpec((tm, n), lambda i: (i, 0))
    return pl.pallas_call(
        functools.partial(_swa_in_kernel, hd=hd),
        grid=(m // tm,),
        in_specs=[row(d), _resident((1, d)), _resident(w.shape), row(LANES), row(LANES)],
        out_specs=[row(nq), row(nk), row(nk)],
        out_shape=[jax.ShapeDtypeStruct((m, nq), BF16), jax.ShapeDtypeStruct((m, nk), F32),
                   jax.ShapeDtypeStruct((m, nk), F32)],
        compiler_params=_params("parallel"),
        name="swa_in",
    )(x, g, w, cos, sin)


def _fox_in_kernel(x_ref, g_ref, w_ref, wf_ref, bf_ref, q_ref, k_ref, v_ref, lf_ref):
    h = _rmsnorm(x_ref[...], g_ref[...]).astype(BF16)
    nq, nk = q_ref.shape[-1], k_ref.shape[-1]
    for s, n in _col_chunks(nq):
        q_ref[:, s:s + n] = _dot(h, w_ref[:, s:s + n]).astype(q_ref.dtype)
    for s, n in _col_chunks(nk):
        k_ref[:, s:s + n] = _dot(h, w_ref[:, nq + s:nq + s + n])
        v_ref[:, s:s + n] = _dot(h, w_ref[:, nq + nk + s:nq + nk + s + n])
    lf_ref[...] = jax.nn.log_sigmoid(_dot(h, wf_ref[...]) + bf_ref[...])


def _fox_in(x, g, w, wf, bf, tm, nq, nk):
    m, d = x.shape
    nh = wf.shape[1]
    row = lambda n: pl.BlockSpec((tm, n), lambda i: (i, 0))
    return pl.pallas_call(
        _fox_in_kernel,
        grid=(m // tm,),
        in_specs=[row(d), _resident((1, d)), _resident(w.shape), _resident(wf.shape), _resident((1, nh))],
        out_specs=[row(nq), row(nk), row(nk), row(nh)],
        out_shape=[jax.ShapeDtypeStruct((m, nq), BF16), jax.ShapeDtypeStruct((m, nk), F32),
                   jax.ShapeDtypeStruct((m, nk), F32), jax.ShapeDtypeStruct((m, nh), F32)],
        compiler_params=_params("parallel"),
        name="fox_in",
    )(x, g, w, wf, bf)


def _out_proj_kernel(a_ref, w_ref, x_ref, o_ref):
    a = a_ref[...]
    for s, n in _col_chunks(o_ref.shape[-1]):
        o_ref[:, s:s + n] = x_ref[:, s:s + n] + _dot(a, w_ref[:, s:s + n])


def _out_proj(a, w, x, tm):
    m, d = x.shape
    k = a.shape[1]
    return pl.pallas_call(
        _out_proj_kernel,
        grid=(m // tm,),
        in_specs=[pl.BlockSpec((tm, k), lambda i: (i, 0)), _resident(w.shape),
                  pl.BlockSpec((tm, d), lambda i: (i, 0))],
        out_specs=pl.BlockSpec((tm, d), lambda i: (i, 0)),
        out_shape=jax.ShapeDtypeStruct((m, d), F32),
        compiler_params=_params("parallel"),
        name="out_proj",
    )(a, w, x)


def _ffn_kernel(x_ref, g_ref, wg_ref, wu_ref, wd_ref, o_ref, h_ref):
    @pl.when(pl.program_id(1) == 0)
    def _():
        x = x_ref[...]
        h_ref[...] = _rmsnorm(x, g_ref[...]).astype(BF16)
        o_ref[...] = x

    h = h_ref[...]
    a = (jax.nn.silu(_dot(h, wg_ref[...])) * _dot(h, wu_ref[...])).astype(BF16)
    o_ref[...] += _dot(a, wd_ref[...])


def _ffn(x, g, wg, wu, wd, tm):
    m, d = x.shape
    f = wg.shape[1]
    tf = _pick_tile(f, FFN_COL_TILE_CAP, LANES)
    return pl.pallas_call(
        _ffn_kernel,
        grid=(m // tm, f // tf),
        in_specs=[pl.BlockSpec((tm, d), lambda i, j: (i, 0)),
                  pl.BlockSpec((1, d), lambda i, j: (0, 0)),
                  pl.BlockSpec((d, tf), lambda i, j: (0, j)),
                  pl.BlockSpec((d, tf), lambda i, j: (0, j)),
                  pl.BlockSpec((tf, d), lambda i, j: (j, 0))],
        out_specs=pl.BlockSpec((tm, d), lambda i, j: (i, 0)),
        out_shape=jax.ShapeDtypeStruct((m, d), F32),
        scratch_shapes=[pltpu.VMEM((tm, d), BF16)],
        compiler_params=_params("parallel", "arbitrary"),
        name="ffn",
    )(x, g, wg, wu, wd)


def _ple_kernel(x_ref, p_ref, g_ref, wp_ref, wg_ref, gf_ref, o_ref, *, final):
    h = _rmsnorm(x_ref[...], g_ref[...]).astype(BF16)
    p = p_ref[...].astype(BF16)
    for s, n in _col_chunks(o_ref.shape[-1]):
        gate = jax.nn.sigmoid(_dot(h, wg_ref[:, s:s + n]))
        o_ref[:, s:s + n] = x_ref[:, s:s + n] + _dot(p, wp_ref[:, s:s + n]) * gate
    if final:
        o_ref[...] = _rmsnorm(o_ref[...], gf_ref[...])


def _ple(x, p, g, wp, wg, gf, tm, final):
    m, d = x.shape
    dp = p.shape[1]
    return pl.pallas_call(
        functools.partial(_ple_kernel, final=final),
        grid=(m // tm,),
        in_specs=[pl.BlockSpec((tm, d), lambda i: (i, 0)), pl.BlockSpec((tm, dp), lambda i: (i, 0)),
                  _resident((1, d)), _resident(wp.shape), _resident(wg.shape), _resident((1, d))],
        out_specs=pl.BlockSpec((tm, d), lambda i: (i, 0)),
        out_shape=jax.ShapeDtypeStruct((m, d), F32),
        compiler_params=_params("parallel"),
        name="ple_final" if final else "ple",
    )(x, p, g, wp, wg, gf)


def _conv_kernel(u_ref, hist_ref, w_ref, b_ref, lg_ref, lb_ref, y_ref, cache_ref, ext_ref, conv_ref, *, tt, kw, pad):
    t = pl.program_id(1)
    off = pad - (kw - 1)

    @pl.when(t == 0)
    def _():
        ext_ref[off:pad, :] = hist_ref[0]

    @pl.when(t > 0)
    def _():
        ext_ref[off:pad, :] = ext_ref[tt + off:tt + pad, :]

    ext_ref[pad:pad + tt, :] = u_ref[0]
    cache_ref[0] = ext_ref[tt + off:tt + pad, :]

    d = ext_ref.shape[-1]
    rc = min(CONV_ROW_CHUNK, tt)
    for cs, cn in _col_chunks(d):
        for r in range(0, tt, rc):
            acc = jnp.broadcast_to(b_ref[:, cs:cs + cn], (rc, cn))
            for k in range(kw):
                acc = acc + ext_ref[off + k + r:off + k + r + rc, cs:cs + cn] * w_ref[k:k + 1, cs:cs + cn]
            conv_ref[r:r + rc, cs:cs + cn] = acc
    y = conv_ref[...]
    mu = jnp.mean(y, axis=-1, keepdims=True)
    yc = y - mu
    var = jnp.mean(yc * yc, axis=-1, keepdims=True)
    z = yc * lax.rsqrt(var + EPS) * lg_ref[...] + lb_ref[...]
    y_ref[0] = jax.nn.silu(z).astype(y_ref.dtype)


def _conv_core(u, hist, w_dw, b_dw, ln_g, ln_b):
    n, t, d = u.shape
    kw = w_dw.shape[0]
    tt = _pick_tile(t, CONV_TIME_TILE_CAP, 8) if t % 8 == 0 else t
    pad = -(-(kw - 1) // 8) * 8
    vec = lambda: pl.BlockSpec((1, d), lambda i, j: (0, 0))
    return pl.pallas_call(
        functools.partial(_conv_kernel, tt=tt, kw=kw, pad=pad),
        grid=(n, t // tt),
        in_specs=[pl.BlockSpec((1, tt, d), lambda i, j: (i, j, 0)),
                  pl.BlockSpec((1, kw - 1, d), lambda i, j: (i, 0, 0)),
                  pl.BlockSpec((kw, d), lambda i, j: (0, 0)), vec(), vec(), vec()],
        out_specs=[pl.BlockSpec((1, tt, d), lambda i, j: (i, j, 0)),
                   pl.BlockSpec((1, kw - 1, d), lambda i, j: (i, 0, 0))],
        out_shape=[jax.ShapeDtypeStruct((n, t, d), BF16), jax.ShapeDtypeStruct((n, kw - 1, d), F32)],
        scratch_shapes=[pltpu.VMEM((pad + tt, d), F32), pltpu.VMEM((tt, d), F32)],
        compiler_params=_params("parallel", "arbitrary"),
        name="conv_core",
    )(u, hist, w_dw, b_dw, ln_g, ln_b)


def _swa_prompt_kernel(sink_ref, q_ref, kp_ref, kc_ref, vp_ref, vc_ref, o_ref, *, n_heads, n_kv, hd, window):
    i = pl.program_id(1)
    tq = q_ref.shape[0]
    grp = n_heads // n_kv
    r = lax.broadcasted_iota(jnp.int32, (tq, 2 * tq), 0)
    c = lax.broadcasted_iota(jnp.int32, (tq, 2 * tq), 1)
    dist = r + tq - c
    valid = (dist >= 0) & (dist <= window) & ((c >= tq) | (i > 0))
    scale = hd ** -0.5
    for kv in range(n_kv):
        ks = slice(kv * hd, (kv + 1) * hd)
        kcat = jnp.concatenate([kp_ref[:, ks], kc_ref[:, ks]], axis=0).astype(BF16)
        vcat = jnp.concatenate([vp_ref[:, ks], vc_ref[:, ks]], axis=0).astype(BF16)
        for g in range(grp):
            h = kv * grp + g
            s = _dot_nt(q_ref[:, h * hd:(h + 1) * hd], kcat) * scale
            s = jnp.where(valid, s, NEG)
            sink = sink_ref[h]
            m = jnp.maximum(jnp.max(s, axis=-1, keepdims=True), sink)
            e = jnp.exp(s - m)
            den = jnp.sum(e, axis=-1, keepdims=True) + jnp.exp(sink - m)
            p = (e / den).astype(BF16)
            o_ref[:, h * hd:(h + 1) * hd] = _dot(p, vcat).astype(o_ref.dtype)


def _swa_prompt(q, k, v, sinks, n_seq, seq, rows_out, n_kv, hd, window):
    tq = ATTN_BLOCK
    assert seq % tq == 0 and window <= tq
    nb = seq // tq
    nq, nk = q.shape[1], k.shape[1]
    cur = lambda n: pl.BlockSpec((tq, n), lambda b, i: (b * nb + i, 0))
    prev = lambda n: pl.BlockSpec((tq, n), lambda b, i: (jnp.maximum(b * nb + i - 1, 0), 0))
    return pl.pallas_call(
        functools.partial(_swa_prompt_kernel, n_heads=nq // hd, n_kv=n_kv, hd=hd, window=window),
        grid=(n_seq, nb),
        in_specs=[pl.BlockSpec(memory_space=pltpu.SMEM), cur(nq), prev(nk), cur(nk), prev(nk), cur(nk)],
        out_specs=cur(nq),
        out_shape=jax.ShapeDtypeStruct((rows_out, nq), BF16),
        compiler_params=_params("parallel", "arbitrary"),
        name="swa_prompt",
    )(sinks, q, k, k, v, v)


def _swa_sample_kernel(q_ref, sink_ref, kc_ref, kn_ref, vc_ref, vn_ref, o_ref, ko_ref, vo_ref, *, hd, window, t_new,
                       past_len):
    nseq, n_kv, rows, _ = q_ref.shape
    nbuf = kc_ref.shape[1]
    ts = nbuf + t_new
    r = lax.broadcasted_iota(jnp.int32, (rows, ts), 0)
    c = lax.broadcasted_iota(jnp.int32, (rows, ts), 1)
    dist = (r % t_new) + nbuf - c
    valid = (dist >= 0) & (dist <= window) & (c >= nbuf - past_len)
    scale = hd ** -0.5
    for n in range(nseq):
        kall = jnp.concatenate([kc_ref[n], kn_ref[n]], axis=0)
        vall = jnp.concatenate([vc_ref[n], vn_ref[n]], axis=0)
        ko_ref[n] = kall[ts - nbuf:]
        vo_ref[n] = vall[ts - nbuf:]
        kb = kall.astype(BF16)
        vb = vall.astype(BF16)
        for kv in range(n_kv):
            s = _dot_nt(q_ref[n, kv], kb[:, kv * hd:(kv + 1) * hd]) * scale
            s = jnp.where(valid, s, NEG)
            sink = sink_ref[kv]
            m = jnp.maximum(jnp.max(s, axis=-1, keepdims=True), sink)
            e = jnp.exp(s - m)
            den = jnp.sum(e, axis=-1, keepdims=True) + jnp.exp(sink - m)
            p = (e / den).astype(BF16)
            o_ref[n, kv] = _dot(p, vb[:, kv * hd:(kv + 1) * hd]).astype(o_ref.dtype)


def _swa_sample(q, k_new, v_new, k_cache, v_cache, sinks, n_kv, hd, window, past_len):
    n, t, nq = q.shape
    grp = nq // hd // n_kv
    nbuf = k_cache.shape[1]
    nk = n_kv * hd
    rows = grp * t
    sb = _pick_tile(n, 8, 1)
    qr = q.reshape(n, t, n_kv, grp, hd).transpose(0, 2, 3, 1, 4).reshape(n, n_kv, rows, hd)
    sink_rows = jnp.broadcast_to(sinks.reshape(n_kv, grp, 1, 1), (n_kv, grp, t, 1)).reshape(n_kv, rows, 1)
    seq3 = lambda a, b: pl.BlockSpec((sb, a, b), lambda i: (i, 0, 0))
    seq4 = pl.BlockSpec((sb, n_kv, rows, hd), lambda i: (i, 0, 0, 0))
    o, ko, vo = pl.pallas_call(
        functools.partial(_swa_sample_kernel, hd=hd, window=window, t_new=t, past_len=past_len),
        grid=(n // sb,),
        in_specs=[seq4, _resident((n_kv, rows, 1)), seq3(nbuf, nk), seq3(t, nk), seq3(nbuf, nk), seq3(t, nk)],
        out_specs=[seq4, seq3(nbuf, nk), seq3(nbuf, nk)],
        out_shape=[jax.ShapeDtypeStruct((n, n_kv, rows, hd), BF16), jax.ShapeDtypeStruct((n, nbuf, nk), F32),
                   jax.ShapeDtypeStruct((n, nbuf, nk), F32)],
        compiler_params=_params("parallel"),
        name="swa_sample",
    )(qr, sink_rows, k_cache, k_new, v_cache, v_new)
    o = o.reshape(n, n_kv, grp, t, hd).transpose(0, 3, 1, 2, 4).reshape(n, t, nq)
    return o, ko, vo


def _tri3(n, kind):
    r = lax.broadcasted_iota(jnp.int32, (n, 3 * n), 0)
    c = lax.broadcasted_iota(jnp.int32, (n, 3 * n), 1) % n
    keep = (c <= r) if kind == "lower_incl" else (c > r)
    return jnp.where(keep, 1.0, 0.0).astype(BF16)


def _tri_sums(tri3, x):
    return _dot(tri3, jnp.concatenate(_split3(x), axis=0))


def _fox_cumsum_kernel(lf_ref, c_ref, *, blk):
    s, nh = lf_ref.shape
    tri = _tri3(blk, "lower_incl")

    def body(j, carry):
        r0 = pl.multiple_of(j * blk, blk)
        cs = _tri_sums(tri, lf_ref[pl.ds(r0, blk), :]) + carry
        c_ref[pl.ds(r0, blk), :] = cs
        return cs[blk - 1:blk, :]

    lax.fori_loop(0, s // blk, body, jnp.zeros((1, nh), F32))


def _fox_cumsum(lf, n_seq, seq):
    nh = lf.shape[1]
    blk = ATTN_BLOCK
    assert seq % blk == 0
    return pl.pallas_call(
        functools.partial(_fox_cumsum_kernel, blk=blk),
        grid=(n_seq,),
        in_specs=[pl.BlockSpec((seq, nh), lambda b: (b, 0))],
        out_specs=pl.BlockSpec((seq, nh), lambda b: (b, 0)),
        out_shape=jax.ShapeDtypeStruct((n_seq * seq, nh), F32),
        compiler_params=_params("parallel"),
        name="fox_cumsum",
    )(lf)


def _fox_prompt_kernel(q_ref, k_ref, v_ref, c_ref, ct_ref, o_ref, m_ref, l_ref, acc_ref, *, grp, hd):
    kv = pl.program_id(1)
    i = pl.program_id(2)
    tq = q_ref.shape[0]
    nh = c_ref.shape[1]
    scale = hd ** -0.5
    head_lane = lax.broadcasted_iota(jnp.int32, (tq, nh), 1)
    cq = [jnp.sum(jnp.where(head_lane == kv * grp + g, c_ref[...], 0.0), axis=-1, keepdims=True) for g in range(grp)]
    r = lax.broadcasted_iota(jnp.int32, (tq, tq), 0)
    c = lax.broadcasted_iota(jnp.int32, (tq, tq), 1)
    causal = r >= c

    m_ref[...] = jnp.full(m_ref.shape, NEG, F32)
    l_ref[...] = jnp.zeros(l_ref.shape, F32)
    acc_ref[...] = jnp.zeros(acc_ref.shape, F32)

    def block(j, masked):
        k0 = pl.multiple_of(j * tq, tq)
        kb = k_ref[pl.ds(k0, tq), :].astype(BF16)
        vb = v_ref[pl.ds(k0, tq), :].astype(BF16)
        for g in range(grp):
            ck = ct_ref[0, 0, g:g + 1, pl.ds(k0, tq)]
            s = _dot_nt(q_ref[:, g * hd:(g + 1) * hd], kb) * scale + (cq[g] - ck)
            if masked:
                s = jnp.where(causal, s, NEG)
            m_old = m_ref[g]
            m_new = jnp.maximum(m_old, jnp.max(s, axis=-1, keepdims=True))
            alpha = jnp.exp(m_old - m_new)
            p = jnp.exp(s - m_new)
            l_ref[g] = alpha * l_ref[g] + jnp.sum(p, axis=-1, keepdims=True)
            acc_ref[g] = alpha * acc_ref[g] + _dot(p.astype(BF16), vb)
            m_ref[g] = m_new

    def body(j, carry):
        block(j, False)
        return carry

    lax.fori_loop(0, i, body, 0)
    block(i, True)
    for g in range(grp):
        o_ref[:, g * hd:(g + 1) * hd] = (acc_ref[g] / l_ref[g]).astype(o_ref.dtype)


def _fox_prompt(q, k, v, c, ct, n_seq, seq, rows_out, n_kv, hd):
    tq = ATTN_BLOCK
    nb = seq // tq
    nq = q.shape[1]
    nh = c.shape[1]
    grp = nq // hd // n_kv
    assert hd % LANES == 0
    return pl.pallas_call(
        functools.partial(_fox_prompt_kernel, grp=grp, hd=hd),
        grid=(n_seq, n_kv, nb),
        in_specs=[pl.BlockSpec((tq, grp * hd), lambda b, kv, i: (b * nb + i, kv)),
                  pl.BlockSpec((seq, hd), lambda b, kv, i: (b, kv)),
                  pl.BlockSpec((seq, hd), lambda b, kv, i: (b, kv)),
                  pl.BlockSpec((tq, nh), lambda b, kv, i: (b * nb + i, 0)),
                  pl.BlockSpec((1, 1, grp, seq), lambda b, kv, i: (b, kv, 0, 0))],
        out_specs=pl.BlockSpec((tq, grp * hd), lambda b, kv, i: (b * nb + i, kv)),
        out_shape=jax.ShapeDtypeStruct((rows_out, nq), BF16),
        scratch_shapes=[pltpu.VMEM((grp, tq, 1), F32), pltpu.VMEM((grp, tq, 1), F32),
                        pltpu.VMEM((grp, tq, hd), F32)],
        compiler_params=_params("parallel", "parallel", "arbitrary"),
        name="fox_prompt",
    )(q, k, v, c, ct)


def _fox_sample_kernel(pt_ref, q_ref, kn_ref, vn_ref, lfn_ref, *refs, n_pg, t_new, nh, scale):
    k_refs, v_refs, lf_refs = refs[:n_pg], refs[n_pg:2 * n_pg], refs[2 * n_pg:3 * n_pg]
    o_ref, m_ref, l_ref, acc_ref, run_ref, cn_ref = refs[3 * n_pg:]
    j = pl.program_id(1)
    pg = kn_ref.shape[1]
    ncols = t_new * nh
    assert 2 * ncols == LANES and pg == LANES
    lane = lax.broadcasted_iota(jnp.int32, (pg, LANES), 1)
    row = lax.broadcasted_iota(jnp.int32, (pg, LANES), 0)
    low = lane < ncols
    lane1 = lax.broadcasted_iota(jnp.int32, (1, LANES), 1)
    diag = row == lane
    qb = q_ref[0]

    def widen(lf_a, lf_b):
        return jnp.concatenate([lf_a] * t_new + [lf_b] * t_new, axis=-1)

    def to_rows(x):
        return jnp.sum(jnp.where(diag, x, 0.0), axis=-1, keepdims=True)

    def accumulate(s_tiles, v_pairs):
        m_old = m_ref[...]
        mx = m_old
        for s in s_tiles:
            mx = jnp.maximum(mx, jnp.max(s, axis=0, keepdims=True))
        m_new = jnp.maximum(mx, pltpu.roll(mx, ncols, 1))
        alpha = jnp.exp(m_old - m_new)
        l_new = alpha * l_ref[...]
        acc = to_rows(alpha)[:ncols] * acc_ref[...]
        for s, (va, vb) in zip(s_tiles, v_pairs):
            p = jnp.exp(s - m_new)
            l_new = l_new + jnp.sum(p, axis=0, keepdims=True)
            pt = p.T.astype(BF16)
            acc = acc + _dot(pt[:ncols], va) + _dot(pt[ncols:], vb)
        m_ref[...] = m_new
        l_ref[...] = l_new
        acc_ref[...] = acc

    @pl.when(j == 0)
    def _():
        lfw = widen(lfn_ref[0], lfn_ref[0])
        cnk = _tri_sums(_tri3(pg, "lower_incl"), lfw)
        tok = (lane % ncols) // nh
        cn = jnp.sum(jnp.where(row == tok, cnk, 0.0), axis=0, keepdims=True)
        cn_ref[...] = cn
        run_ref[...] = jnp.zeros(run_ref.shape, F32)
        m_ref[...] = jnp.full(m_ref.shape, NEG, F32)
        l_ref[...] = jnp.zeros(l_ref.shape, F32)
        acc_ref[...] = jnp.zeros(acc_ref.shape, F32)
        s = _dot(kn_ref[0].astype(BF16), qb) * scale + (cn - cnk)
        s = jnp.where((row <= tok) & low, s, NEG)
        vn = vn_ref[0].astype(BF16)
        accumulate([s], [(vn, vn)])

    tri = _tri3(pg, "upper_excl")
    s_tiles, v_pairs = [], []
    run = run_ref[...]
    for a in range(0, n_pg, 2):
        sa = _dot(k_refs[a][0].astype(BF16), qb)
        sb = _dot(k_refs[a + 1][0].astype(BF16), qb)
        lfw = widen(lf_refs[a][0], lf_refs[a + 1][0])
        tot = jnp.sum(lfw, axis=0, keepdims=True)
        tot_sw = pltpu.roll(tot, ncols, 1)
        bias = _tri_sums(tri, lfw) + run + jnp.where(lane1 >= ncols, tot_sw, 0.0) + cn_ref[...]
        s_tiles.append(jnp.where(low, sa, sb) * scale + bias)
        v_pairs.append((v_refs[a][0].astype(BF16), v_refs[a + 1][0].astype(BF16)))
        run = run + tot + tot_sw
    run_ref[...] = run
    accumulate(s_tiles, v_pairs)

    @pl.when(j == pl.num_programs(1) - 1)
    def _():
        l_tot = l_ref[...] + pltpu.roll(l_ref[...], ncols, 1)
        o_ref[0] = acc_ref[...] / to_rows(l_tot)[:ncols]


def _fox_sample(q, k_new, v_new, lf_new, cache_k, cache_v, cache_lf, page_table, n_kv, hd):
    n, t, nq = q.shape
    nh = nq // hd
    grp = nh // n_kv
    nk = n_kv * hd
    n_pool, pg = cache_k.shape[0], cache_k.shape[1]
    n_pages = page_table.shape[1]
    ncols = t * nh
    n_pg = _pick_tile(n_pages, FOX_PAGES_PER_STEP, 2)
    assert n_pages % n_pg == 0 and n_pg % 2 == 0
    qt = q.reshape(n, t, n_kv, grp, hd).transpose(0, 4, 1, 2, 3)
    own = jnp.eye(n_kv, dtype=bool)[None, :, None, None, :, None]
    qb = jnp.where(own, qt[:, None], jnp.zeros((), q.dtype)).reshape(n, nk, ncols)
    qb = jnp.concatenate([qb, qb], axis=-1)
    padrows = lambda a: jnp.pad(a, ((0, 0), (0, pg - t), (0, 0)))
    kc = cache_k.reshape(n_pool, pg, nk)
    vc = cache_v.reshape(n_pool, pg, nk)

    def page(width, p):
        return pl.BlockSpec((1, pg, width), lambda i, j, pt: (pt[i, n_pages - 1 - (j * n_pg + p)], 0, 0))

    per_seq = lambda a, b: pl.BlockSpec((1, a, b), lambda i, j, pt: (i, 0, 0))
    o = pl.pallas_call(
        functools.partial(_fox_sample_kernel, n_pg=n_pg, t_new=t, nh=nh, scale=hd ** -0.5),
        grid_spec=pltpu.PrefetchScalarGridSpec(
            num_scalar_prefetch=1,
            grid=(n, n_pages // n_pg),
            in_specs=[per_seq(nk, 2 * ncols), per_seq(pg, nk), per_seq(pg, nk), per_seq(pg, nh)]
            + [page(nk, p) for p in range(n_pg)] + [page(nk, p) for p in range(n_pg)]
            + [page(nh, p) for p in range(n_pg)],
            out_specs=per_seq(ncols, nk),
            scratch_shapes=[pltpu.VMEM((1, LANES), F32), pltpu.VMEM((1, LANES), F32), pltpu.VMEM((ncols, nk), F32),
                            pltpu.VMEM((1, LANES), F32), pltpu.VMEM((1, LANES), F32)]),
        out_shape=jax.ShapeDtypeStruct((n, ncols, nk), F32),
        compiler_params=_params("parallel", "arbitrary"),
        name="fox_sample",
    )(page_table, qb, padrows(k_new), padrows(v_new), padrows(lf_new), *([kc] * n_pg), *([vc] * n_pg),
      *([cache_lf] * n_pg))
    o = o.reshape(n, t, n_kv, grp, n_kv, hd)
    return jnp.stack([o[:, :, kv, :, kv, :] for kv in range(n_kv)], axis=2).reshape(n, t, nq)


def _rope_tables(pos, hd):
    half = hd // 2
    inv = ROPE_THETA ** (-2.0 * jnp.arange(half, dtype=F32) / hd)
    ang = pos.astype(F32)[:, None] * inv[None, :]
    cos = jnp.concatenate([jnp.cos(ang), jnp.cos(ang)], axis=-1)
    sin = jnp.concatenate([-jnp.sin(ang), jnp.sin(ang)], axis=-1)
    reps = LANES // hd
    return jnp.tile(cos, (1, reps)), jnp.tile(sin, (1, reps))


def kernel(x_prompt, x_sample, p_prompt, p_sample, cache_conv, cache_swa_k, cache_swa_v, cache_fox_k, cache_fox_v, cache_fox_logf, page_table, norm_mix, norm_ffn, norm_ple, norm_final, conv_w_in, conv_w_dw, conv_b_dw, conv_ln_g, conv_ln_b, conv_w_out, swa_w_qkv, swa_sinks, swa_w_o, fox_w_in, fox_b_f, fox_w_o, ffn_w_gate, ffn_w_up, ffn_w_down, ple_w_proj, ple_w_gate):
    nb, seq, d = x_prompt.shape
    ns, ts, _ = x_sample.shape
    depth = norm_mix.shape[0]
    mp, ms = nb * seq, ns * ts
    m = mp + ms
    tm = _pick_tile(m, ROW_TILE_CAP, BF16_SUBLANES)
    kw = conv_w_dw.shape[1]
    window, swa_kv, swa_hd = cache_swa_k.shape[2:]
    swa_nq, swa_nk = swa_sinks.shape[1] * swa_hd, swa_kv * swa_hd
    fox_kv, fox_hd = cache_fox_k.shape[3:]
    fox_h = fox_b_f.shape[1]
    fox_nq, fox_nk = fox_h * fox_hd, fox_kv * fox_hd
    past_len = page_table.shape[1] * cache_fox_k.shape[2]

    x = jnp.concatenate([x_prompt.reshape(mp, d), x_sample.reshape(ms, d)], axis=0)
    pos = jnp.concatenate([jnp.tile(jnp.arange(seq, dtype=jnp.int32), nb),
                           jnp.tile(past_len + jnp.arange(ts, dtype=jnp.int32), ns)])
    cos, sin = _rope_tables(pos, swa_hd)
    vec = lambda a: a.reshape(1, -1)
    merged = lambda a_p, a_s: lax.dynamic_update_slice(a_p, a_s.reshape(ms, -1), (mp, 0))

    conv_p, conv_s = [], []
    swk_p, swv_p, swk_s, swv_s = [], [], [], []
    fk_p, fv_p, fl_p, fk_s, fv_s, fl_s = [], [], [], [], [], []
    for i in range(depth):
        kind, j = i % 3, i // 3
        g_mix = vec(norm_mix[i])
        if kind == 0:
            u = _conv_in(x, g_mix, conv_w_in[j].astype(BF16), tm)
            dc = u.shape[1]
            conv_w = (conv_w_dw[j], vec(conv_b_dw[j]), vec(conv_ln_g[j]), vec(conv_ln_b[j]))
            u_p, u_s = u[:mp].reshape(nb, seq, dc), u[mp:].reshape(ns, ts, dc)
            y_p, c_p = _conv_core(u_p, jnp.zeros((nb, kw - 1, dc), F32), *conv_w)
            y_s, c_s = _conv_core(u_s, cache_conv[j], *conv_w)
            conv_p.append(c_p)
            conv_s.append(c_s)
            mixed = jnp.concatenate([y_p.reshape(mp, dc), y_s.reshape(ms, dc)], axis=0)
            w_o = conv_w_out[j]
        elif kind == 1:
            q, k, v = _swa_in(x, g_mix, swa_w_qkv[j].astype(BF16), cos, sin, tm, swa_nq, swa_nk, swa_hd)
            o_p = _swa_prompt(q, k, v, swa_sinks[j], nb, seq, m, swa_kv, swa_hd, window)
            o_s, ks_, vs_ = _swa_sample(q[mp:].reshape(ns, ts, swa_nq), k[mp:].reshape(ns, ts, swa_nk),
                                        v[mp:].reshape(ns, ts, swa_nk), cache_swa_k[j].reshape(ns, window, swa_nk),
                                        cache_swa_v[j].reshape(ns, window, swa_nk), swa_sinks[j], swa_kv, swa_hd,
                                        window, past_len)
            k_p, v_p = k[:mp].reshape(nb, seq, swa_kv, swa_hd), v[:mp].reshape(nb, seq, swa_kv, swa_hd)
            swk_p.append(k_p[:, seq - window:])
            swv_p.append(v_p[:, seq - window:])
            swk_s.append(ks_.reshape(ns, window, swa_kv, swa_hd))
            swv_s.append(vs_.reshape(ns, window, swa_kv, swa_hd))
            mixed = merged(o_p, o_s)
            w_o = swa_w_o[j]
        else:
            w_in = fox_w_in[j]
            q, k, v, lf = _fox_in(x, g_mix, w_in[:, :fox_nq + 2 * fox_nk].astype(BF16),
                                  w_in[:, fox_nq + 2 * fox_nk:].astype(BF16), vec(fox_b_f[j]), tm, fox_nq, fox_nk)
            c = _fox_cumsum(lf, nb, seq)
            ct = c.reshape(nb, seq, fox_kv, fox_h // fox_kv).transpose(0, 2, 3, 1)
            o_p = _fox_prompt(q, k, v, c, ct, nb, seq, m, fox_kv, fox_hd)
            o_s = _fox_sample(q[mp:].reshape(ns, ts, fox_nq), k[mp:].reshape(ns, ts, fox_nk),
                              v[mp:].reshape(ns, ts, fox_nk), lf[mp:].reshape(ns, ts, fox_h),
                              cache_fox_k[j], cache_fox_v[j], cache_fox_logf[j], page_table, fox_kv, fox_hd)
            fk_p.append(k[:mp].reshape(nb, seq, fox_kv, fox_hd))
            fv_p.append(v[:mp].reshape(nb, seq, fox_kv, fox_hd))
            fl_p.append(lf[:mp].reshape(nb, seq, fox_h))
            fk_s.append(k[mp:].reshape(ns, ts, fox_kv, fox_hd))
            fv_s.append(v[mp:].reshape(ns, ts, fox_kv, fox_hd))
            fl_s.append(lf[mp:].reshape(ns, ts, fox_h))
            mixed = merged(o_p, o_s.astype(BF16))
            w_o = fox_w_o[j]
        x = _out_proj(mixed, w_o.astype(BF16), x, tm)
        x = _ffn(x, vec(norm_ffn[i]), ffn_w_gate[i].astype(BF16), ffn_w_up[i].astype(BF16),
                 ffn_w_down[i].astype(BF16), tm)
        p = jnp.concatenate([p_prompt[i].reshape(mp, -1), p_sample[i].reshape(ms, -1)], axis=0)
        x = _ple(x, p, vec(norm_ple[i]), ple_w_proj[i].astype(BF16), ple_w_gate[i].astype(BF16), vec(norm_final),
                 tm, final=(i == depth - 1))

    return (x[:mp].reshape(nb, seq, d), x[mp:].reshape(ns, ts, d),
            jnp.stack(conv_p), jnp.stack(conv_s),
            jnp.stack(swk_p), jnp.stack(swv_p), jnp.stack(swk_s), jnp.stack(swv_s),
            jnp.stack(fk_p), jnp.stack(fv_p), jnp.stack(fl_p),
            jnp.stack(fk_s), jnp.stack(fv_s), jnp.stack(fl_s))
```

```python
import functools

import jax
import jax.numpy as jnp
from jax import lax
from jax.experimental import pallas as pl
from jax.experimental.pallas import tpu as pltpu

EPS = 1e-6
NEG = -1e30
ROPE_THETA = 10000.0
F32 = jnp.float32
BF16 = jnp.bfloat16

LANES = 128
SUBLANES = 8
BF16_SUBLANES = 16
VMEM_LIMIT = 56 * 1024 * 1024
ROW_TILE_CAP = 640
FFN_COL_TILE_CAP = 512
COL_CHUNK = 512
ATTN_BLOCK = 128
FOX_BLOCK_CAP = 256
CONV_TIME_TILE_CAP = 128
CONV_ROW_CHUNK = 32
FOX_PAGES_PER_STEP = 8


def _pick_tile(n, cap, mult):
    best = None
    for d in range(mult, min(n, cap) + 1, mult):
        if n % d == 0:
            best = d
    return best if best is not None else n


def _params(*sem):
    return pltpu.CompilerParams(dimension_semantics=sem, vmem_limit_bytes=VMEM_LIMIT)


def _resident(shape):
    nd = len(shape)
    return pl.BlockSpec(shape, lambda *_: (0,) * nd, pipeline_mode=pl.Buffered(1))


def _rmsnorm(x, g):
    return x * lax.rsqrt(jnp.mean(x * x, axis=-1, keepdims=True) + EPS) * g


def _dot(a, b):
    return jnp.dot(a, b, preferred_element_type=F32)


def _dot_nt(a, b):
    return lax.dot_general(a, b, (((1,), (1,)), ((), ())), preferred_element_type=F32)


def _col_chunks(n, chunk=COL_CHUNK):
    return [(s, min(chunk, n - s)) for s in range(0, n, chunk)]


def _split3(x):
    hi = x.astype(BF16)
    r1 = x - hi.astype(F32)
    mid = r1.astype(BF16)
    lo = (r1 - mid.astype(F32)).astype(BF16)
    return hi, mid, lo


def _conv_in_kernel(x_ref, g_ref, w_ref, u_ref):
    h = _rmsnorm(x_ref[...], g_ref[...]).astype(BF16)
    dc = u_ref.shape[-1]
    for s, n in _col_chunks(dc):
        a = _dot(h, w_ref[:, s:s + n])
        b = _dot(h, w_ref[:, dc + s:dc + s + n])
        u_ref[:, s:s + n] = a * jax.nn.sigmoid(b)


def _conv_in(x, g, w, tm):
    m, d = x.shape
    dc = w.shape[1] // 2
    return pl.pallas_call(
        _conv_in_kernel,
        grid=(m // tm,),
        in_specs=[pl.BlockSpec((tm, d), lambda i: (i, 0)), _resident((1, d)), _resident(w.shape)],
        out_specs=pl.BlockSpec((tm, dc), lambda i: (i, 0)),
        out_shape=jax.ShapeDtypeStruct((m, dc), F32),
        compiler_params=_params("parallel"),
        name="conv_in",
    )(x, g, w)


def _swa_in_kernel(x_ref, g_ref, w_ref, cos_ref, sin_ref, q_ref, k_ref, v_ref, *, hd):
    h = _rmsnorm(x_ref[...], g_ref[...]).astype(BF16)
    cos = cos_ref[...]
    sin = sin_ref[...]
    lane = lax.broadcasted_iota(jnp.int32, cos.shape, 1)
    first_half = (lane % hd) < hd // 2

    def rope(z):
        partner = jnp.where(first_half, pltpu.roll(z, LANES - hd // 2, 1), pltpu.roll(z, hd // 2, 1))
        return z * cos + partner * sin

    nq, nk = q_ref.shape[-1], k_ref.shape[-1]
    for s, n in _col_chunks(nq):
        z = _dot(h, w_ref[:, s:s + n])
        for c in range(0, n, LANES):
            q_ref[:, s + c:s + c + LANES] = rope(z[:, c:c + LANES]).astype(q_ref.dtype)
    for s, n in _col_chunks(nk):
        z = _dot(h, w_ref[:, nq + s:nq + s + n])
        for c in range(0, n, LANES):
            k_ref[:, s + c:s + c + LANES] = rope(z[:, c:c + LANES])
    v_ref[...] = _dot(h, w_ref[:, nq + nk:])


def _swa_in(x, g, w, cos, sin, tm, nq, nk, hd):
    m, d = x.shape
    assert LANES % hd == 0 and nq % LANES == 0 and nk % LANES == 0
    row = lambda n: pl.BlockSpec((tm, n), lambda i: (i, 0))
    return pl.pallas_call(
        functools.partial(_swa_in_kernel, hd=hd),
        grid=(m // tm,),
        in_specs=[row(d), _resident((1, d)), _resident(w.shape), row(LANES), row(LANES)],
        out_specs=[row(nq), row(nk), row(nk)],
        out_shape=[jax.ShapeDtypeStruct((m, nq), BF16), jax.ShapeDtypeStruct((m, nk), F32),
                   jax.ShapeDtypeStruct((m, nk), F32)],
        compiler_params=_params("parallel"),
        name="swa_in",
    )(x, g, w, cos, sin)


def _fox_in_kernel(x_ref, g_ref, w_ref, wf_ref, bf_ref, q_ref, k_ref, v_ref, lf_ref):
    h = _rmsnorm(x_ref[...], g_ref[...]).astype(BF16)
    nq, nk = q_ref.shape[-1], k_ref.shape[-1]
    for s, n in _col_chunks(nq):
        q_ref[:, s:s + n] = _dot(h, w_ref[:, s:s + n]).astype(q_ref.dtype)
    for s, n in _col_chunks(nk):
        k_ref[:, s:s + n] = _dot(h, w_ref[:, nq + s:nq + s + n])
        v_ref[:, s:s + n] = _dot(h, w_ref[:, nq + nk + s:nq + nk + s + n])
    lf_ref[...] = jax.nn.log_sigmoid(_dot(h, wf_ref[...]) + bf_ref[...])


def _fox_in(x, g, w, wf, bf, tm, nq, nk):
    m, d = x.shape
    nh = wf.shape[1]
    row = lambda n: pl.BlockSpec((tm, n), lambda i: (i, 0))
    return pl.pallas_call(
        _fox_in_kernel,
        grid=(m // tm,),
        in_specs=[row(d), _resident((1, d)), _resident(w.shape), _resident(wf.shape), _resident((1, nh))],
        out_specs=[row(nq), row(nk), row(nk), row(nh)],
        out_shape=[jax.ShapeDtypeStruct((m, nq), BF16), jax.ShapeDtypeStruct((m, nk), F32),
                   jax.ShapeDtypeStruct((m, nk), F32), jax.ShapeDtypeStruct((m, nh), F32)],
        compiler_params=_params("parallel"),
        name="fox_in",
    )(x, g, w, wf, bf)


def _out_proj_kernel(a_ref, w_ref, x_ref, o_ref):
    a = a_ref[...]
    for s, n in _col_chunks(o_ref.shape[-1]):
        o_ref[:, s:s + n] = x_ref[:, s:s + n] + _dot(a, w_ref[:, s:s + n])


def _out_proj(a, w, x, tm):
    m, d = x.shape
    k = a.shape[1]
    return pl.pallas_call(
        _out_proj_kernel,
        grid=(m // tm,),
        in_specs=[pl.BlockSpec((tm, k), lambda i: (i, 0)), _resident(w.shape),
                  pl.BlockSpec((tm, d), lambda i: (i, 0))],
        out_specs=pl.BlockSpec((tm, d), lambda i: (i, 0)),
        out_shape=jax.ShapeDtypeStruct((m, d), F32),
        compiler_params=_params("parallel"),
        name="out_proj",
    )(a, w, x)


def _ffn_kernel(x_ref, g_ref, wg_ref, wu_ref, wd_ref, o_ref, h_ref):
    @pl.when(pl.program_id(1) == 0)
    def _():
        x = x_ref[...]
        h_ref[...] = _rmsnorm(x, g_ref[...]).astype(BF16)
        o_ref[...] = x

    h = h_ref[...]
    a = (jax.nn.silu(_dot(h, wg_ref[...])) * _dot(h, wu_ref[...])).astype(BF16)
    o_ref[...] += _dot(a, wd_ref[...])


def _ffn(x, g, wg, wu, wd, tm):
    m, d = x.shape
    f = wg.shape[1]
    tf = _pick_tile(f, FFN_COL_TILE_CAP, LANES)
    return pl.pallas_call(
        _ffn_kernel,
        grid=(m // tm, f // tf),
        in_specs=[pl.BlockSpec((tm, d), lambda i, j: (i, 0)),
                  pl.BlockSpec((1, d), lambda i, j: (0, 0)),
                  pl.BlockSpec((d, tf), lambda i, j: (0, j)),
                  pl.BlockSpec((d, tf), lambda i, j: (0, j)),
                  pl.BlockSpec((tf, d), lambda i, j: (j, 0))],
        out_specs=pl.BlockSpec((tm, d), lambda i, j: (i, 0)),
        out_shape=jax.ShapeDtypeStruct((m, d), F32),
        scratch_shapes=[pltpu.VMEM((tm, d), BF16)],
        compiler_params=_params("parallel", "arbitrary"),
        name="ffn",
    )(x, g, wg, wu, wd)


def _ple_kernel(x_ref, p_ref, g_ref, wp_ref, wg_ref, gf_ref, o_ref, *, final):
    h = _rmsnorm(x_ref[...], g_ref[...]).astype(BF16)
    p = p_ref[...].astype(BF16)
    for s, n in _col_chunks(o_ref.shape[-1]):
        gate = jax.nn.sigmoid(_dot(h, wg_ref[:, s:s + n]))
        o_ref[:, s:s + n] = x_ref[:, s:s + n] + _dot(p, wp_ref[:, s:s + n]) * gate
    if final:
        o_ref[...] = _rmsnorm(o_ref[...], gf_ref[...])


def _ple(x, p, g, wp, wg, gf, tm, final):
    m, d = x.shape
    dp = p.shape[1]
    return pl.pallas_call(
        functools.partial(_ple_kernel, final=final),
        grid=(m // tm,),
        in_specs=[pl.BlockSpec((tm, d), lambda i: (i, 0)), pl.BlockSpec((tm, dp), lambda i: (i, 0)),
                  _resident((1, d)), _resident(wp.shape), _resident(wg.shape), _resident((1, d))],
        out_specs=pl.BlockSpec((tm, d), lambda i: (i, 0)),
        out_shape=jax.ShapeDtypeStruct((m, d), F32),
        compiler_params=_params("parallel"),
        name="ple_final" if final else "ple",
    )(x, p, g, wp, wg, gf)


def _conv_kernel(u_ref, hist_ref, w_ref, b_ref, lg_ref, lb_ref, y_ref, cache_ref, ext_ref, sh_ref, wb_ref, conv_ref,
                 *, tt, kw, pad):
    t = pl.program_id(1)
    off = pad - (kw - 1)

    sub = wb_ref.shape[1]

    @pl.when(t == 0)
    def _():
        ext_ref[off:pad, :] = hist_ref[0]
        for k in range(kw):
            wb_ref[k] = jnp.broadcast_to(w_ref[k:k + 1, :], wb_ref.shape[1:])

    @pl.when(t > 0)
    def _():
        ext_ref[off:pad, :] = ext_ref[tt + off:tt + pad, :]

    ext_ref[pad:pad + tt, :] = u_ref[...].reshape(tt, u_ref.shape[-1])
    cache_ref[0] = ext_ref[tt + off:tt + pad, :]

    d = ext_ref.shape[-1]
    rc = min(CONV_ROW_CHUNK, tt)
    rows, cn = sh_ref.shape[1:]
    for cs in range(0, d, cn):
        for b in range(1, SUBLANES):
            sh_ref[b - 1] = ext_ref[b:b + rows, cs:cs + cn]
        for r in range(0, tt, rc):
            accs = [jnp.broadcast_to(b_ref[:, cs:cs + cn], (sub, cn))] * (rc // sub)
            for k in range(kw):
                a, b = divmod(off + k, SUBLANES)
                wk = wb_ref[k, :, cs:cs + cn]
                for q in range(rc // sub):
                    r0 = a * SUBLANES + r + q * sub
                    src = ext_ref[r0:r0 + sub, cs:cs + cn] if b == 0 else sh_ref[b - 1, r0:r0 + sub, :]
                    accs[q] = accs[q] + src * wk
            for q in range(rc // sub):
                conv_ref[r + q * sub:r + (q + 1) * sub, cs:cs + cn] = accs[q]
    y = conv_ref[...]
    mu = jnp.mean(y, axis=-1, keepdims=True)
    yc = y - mu
    var = jnp.mean(yc * yc, axis=-1, keepdims=True)
    z = yc * lax.rsqrt(var + EPS) * lg_ref[...] + lb_ref[...]
    y_ref[...] = jax.nn.silu(z).astype(y_ref.dtype).reshape(y_ref.shape)


def _conv_core(u, hist, w_dw, b_dw, ln_g, ln_b, n, t, rows_out):
    d = u.shape[-1]
    kw = w_dw.shape[0]
    pad = -(-(kw - 1) // SUBLANES) * SUBLANES
    if u.ndim == 2:
        tt = _pick_tile(t, CONV_TIME_TILE_CAP, SUBLANES)
        nt = t // tt
        row_spec = pl.BlockSpec((tt, d), lambda i, j: (i * nt + j, 0))
        y_shape = (rows_out, d)
    else:
        tt = t
        row_spec = pl.BlockSpec((1, t, d), lambda i, j: (i, 0, 0))
        y_shape = (n, t, d)
    assert tt == t or tt >= kw - 1
    assert d % COL_CHUNK == 0
    vec = lambda: pl.BlockSpec((1, d), lambda i, j: (0, 0))
    return pl.pallas_call(
        functools.partial(_conv_kernel, tt=tt, kw=kw, pad=pad),
        grid=(n, t // tt),
        in_specs=[row_spec, pl.BlockSpec((1, kw - 1, d), lambda i, j: (i, 0, 0)),
                  pl.BlockSpec((kw, d), lambda i, j: (0, 0)), vec(), vec(), vec()],
        out_specs=[row_spec, pl.BlockSpec((1, kw - 1, d), lambda i, j: (i, 0, 0))],
        out_shape=[jax.ShapeDtypeStruct(y_shape, BF16), jax.ShapeDtypeStruct((n, kw - 1, d), F32)],
        scratch_shapes=[pltpu.VMEM((pad + tt, d), F32),
                        pltpu.VMEM((SUBLANES - 1, tt + pad - SUBLANES, COL_CHUNK), F32),
                        pltpu.VMEM((kw, min(SUBLANES, tt), d), F32), pltpu.VMEM((tt, d), F32)],
        compiler_params=_params("parallel", "arbitrary"),
        name="conv_core",
    )(u, hist, w_dw, b_dw, ln_g, ln_b)


def _swa_prompt_kernel(sink_ref, q_ref, kp_ref, kc_ref, vp_ref, vc_ref, o_ref, *, n_heads, n_kv, hd, window):
    i = pl.program_id(1)
    tq = q_ref.shape[0]
    grp = n_heads // n_kv
    r = lax.broadcasted_iota(jnp.int32, (tq, 2 * tq), 0)
    c = lax.broadcasted_iota(jnp.int32, (tq, 2 * tq), 1)
    dist = r + tq - c
    valid = (dist >= 0) & (dist <= window) & ((c >= tq) | (i > 0))
    scale = hd ** -0.5
    for kv in range(n_kv):
        ks = slice(kv * hd, (kv + 1) * hd)
        kcat = jnp.concatenate([kp_ref[:, ks], kc_ref[:, ks]], axis=0).astype(BF16)
        vcat = jnp.concatenate([vp_ref[:, ks], vc_ref[:, ks]], axis=0).astype(BF16)
        for g in range(grp):
            h = kv * grp + g
            s = _dot_nt(q_ref[:, h * hd:(h + 1) * hd], kcat) * scale
            s = jnp.where(valid, s, NEG)
            sink = sink_ref[h]
            m = jnp.maximum(jnp.max(s, axis=-1, keepdims=True), sink)
            e = jnp.exp(s - m)
            den = jnp.sum(e, axis=-1, keepdims=True) + jnp.exp(sink - m)
            p = (e / den).astype(BF16)
            o_ref[:, h * hd:(h + 1) * hd] = _dot(p, vcat).astype(o_ref.dtype)


def _swa_prompt(q, k, v, sinks, n_seq, seq, rows_out, n_kv, hd, window):
    tq = ATTN_BLOCK
    assert seq % tq == 0 and window <= tq
    nb = seq // tq
    nq, nk = q.shape[1], k.shape[1]
    cur = lambda n: pl.BlockSpec((tq, n), lambda b, i: (b * nb + i, 0))
    prev = lambda n: pl.BlockSpec((tq, n), lambda b, i: (jnp.maximum(b * nb + i - 1, 0), 0))
    return pl.pallas_call(
        functools.partial(_swa_prompt_kernel, n_heads=nq // hd, n_kv=n_kv, hd=hd, window=window),
        grid=(n_seq, nb),
        in_specs=[pl.BlockSpec(memory_space=pltpu.SMEM), cur(nq), prev(nk), cur(nk), prev(nk), cur(nk)],
        out_specs=cur(nq),
        out_shape=jax.ShapeDtypeStruct((rows_out, nq), BF16),
        compiler_params=_params("parallel", "arbitrary"),
        name="swa_prompt",
    )(sinks, q, k, k, v, v)


def _swa_sample_kernel(q_ref, sink_ref, kc_ref, kn_ref, vc_ref, vn_ref, o_ref, ko_ref, vo_ref, *, hd, window, t_new,
                       past_len):
    nseq, n_kv, rows, _ = q_ref.shape
    nbuf = kc_ref.shape[1]
    ts = nbuf + t_new
    r = lax.broadcasted_iota(jnp.int32, (rows, ts), 0)
    c = lax.broadcasted_iota(jnp.int32, (rows, ts), 1)
    dist = (r % t_new) + nbuf - c
    valid = (dist >= 0) & (dist <= window) & (c >= nbuf - past_len)
    scale = hd ** -0.5
    for n in range(nseq):
        kall = jnp.concatenate([kc_ref[n], kn_ref[n]], axis=0)
        vall = jnp.concatenate([vc_ref[n], vn_ref[n]], axis=0)
        ko_ref[n] = kall[ts - nbuf:]
        vo_ref[n] = vall[ts - nbuf:]
        kb = kall.astype(BF16)
        vb = vall.astype(BF16)
        for kv in range(n_kv):
            s = _dot_nt(q_ref[n, kv], kb[:, kv * hd:(kv + 1) * hd]) * scale
            s = jnp.where(valid, s, NEG)
            sink = sink_ref[kv]
            m = jnp.maximum(jnp.max(s, axis=-1, keepdims=True), sink)
            e = jnp.exp(s - m)
            den = jnp.sum(e, axis=-1, keepdims=True) + jnp.exp(sink - m)
            p = (e / den).astype(BF16)
            o_ref[n, kv] = _dot(p, vb[:, kv * hd:(kv + 1) * hd]).astype(o_ref.dtype)


def _swa_sample(q, k_new, v_new, k_cache, v_cache, sinks, n_kv, hd, window, past_len):
    n, t, nq = q.shape
    grp = nq // hd // n_kv
    nbuf = k_cache.shape[1]
    nk = n_kv * hd
    rows = grp * t
    sb = _pick_tile(n, 8, 1)
    qr = q.reshape(n, t, n_kv, grp, hd).transpose(0, 2, 3, 1, 4).reshape(n, n_kv, rows, hd)
    sink_rows = jnp.broadcast_to(sinks.reshape(n_kv, grp, 1, 1), (n_kv, grp, t, 1)).reshape(n_kv, rows, 1)
    seq3 = lambda a, b: pl.BlockSpec((sb, a, b), lambda i: (i, 0, 0))
    seq4 = pl.BlockSpec((sb, n_kv, rows, hd), lambda i: (i, 0, 0, 0))
    o, ko, vo = pl.pallas_call(
        functools.partial(_swa_sample_kernel, hd=hd, window=window, t_new=t, past_len=past_len),
        grid=(n // sb,),
        in_specs=[seq4, _resident((n_kv, rows, 1)), seq3(nbuf, nk), seq3(t, nk), seq3(nbuf, nk), seq3(t, nk)],
        out_specs=[seq4, seq3(nbuf, nk), seq3(nbuf, nk)],
        out_shape=[jax.ShapeDtypeStruct((n, n_kv, rows, hd), BF16), jax.ShapeDtypeStruct((n, nbuf, nk), F32),
                   jax.ShapeDtypeStruct((n, nbuf, nk), F32)],
        compiler_params=_params("parallel"),
        name="swa_sample",
    )(qr, sink_rows, k_cache, k_new, v_cache, v_new)
    o = o.reshape(n, n_kv, grp, t, hd).transpose(0, 3, 1, 2, 4).reshape(n, t, nq)
    return o, ko, vo


def _tri3(n, kind):
    r = lax.broadcasted_iota(jnp.int32, (n, 3 * n), 0)
    c = lax.broadcasted_iota(jnp.int32, (n, 3 * n), 1) % n
    keep = (c <= r) if kind == "lower_incl" else (c > r)
    return jnp.where(keep, 1.0, 0.0).astype(BF16)


def _tri_sums(tri3, x):
    return _dot(tri3, jnp.concatenate(_split3(x), axis=0))


def _fox_cumsum_kernel(lf_ref, c_ref, *, blk):
    s, nh = lf_ref.shape
    tri = _tri3(blk, "lower_incl")

    def body(j, carry):
        r0 = pl.multiple_of(j * blk, blk)
        cs = _tri_sums(tri, lf_ref[pl.ds(r0, blk), :]) + carry
        c_ref[pl.ds(r0, blk), :] = cs
        return cs[blk - 1:blk, :]

    lax.fori_loop(0, s // blk, body, jnp.zeros((1, nh), F32))


def _fox_cumsum(lf, n_seq, seq):
    nh = lf.shape[1]
    blk = ATTN_BLOCK
    assert seq % blk == 0
    return pl.pallas_call(
        functools.partial(_fox_cumsum_kernel, blk=blk),
        grid=(n_seq,),
        in_specs=[pl.BlockSpec((seq, nh), lambda b: (b, 0))],
        out_specs=pl.BlockSpec((seq, nh), lambda b: (b, 0)),
        out_shape=jax.ShapeDtypeStruct((n_seq * seq, nh), F32),
        compiler_params=_params("parallel"),
        name="fox_cumsum",
    )(lf)


def _fox_prompt_kernel(q_ref, k_ref, v_ref, c_ref, ct_ref, o_ref, kb_ref, vt_ref, ckb_ref, m_ref, l_ref, acc_ref, *,
                       grp, hd):
    kv = pl.program_id(1)
    i = pl.program_id(2)
    tq = q_ref.shape[0]
    seq, nh = c_ref.shape
    scale = hd ** -0.5

    @pl.when(i == 0)
    def _():
        kb_ref[...] = k_ref[...].astype(BF16)
        for b in range(0, seq, LANES):
            vt_ref[:, b:b + LANES] = v_ref[b:b + LANES, :].T.astype(BF16)
        parts = _split3(c_ref[...])
        head = lax.broadcasted_iota(jnp.int32, (nh, LANES), 0)
        for g in range(grp):
            sel = jnp.where(head == kv * grp + g, 1.0, 0.0).astype(BF16)
            ckb_ref[g] = _dot(parts[0], sel) + _dot(parts[1], sel) + _dot(parts[2], sel)

    q0 = pl.multiple_of(i * tq, tq)
    qs = jnp.concatenate([q_ref[:, g * hd:(g + 1) * hd] for g in range(grp)], axis=0)
    cq = jnp.concatenate([ct_ref[0, 0, g:g + 1, pl.ds(q0, tq)] for g in range(grp)], axis=1)
    key_row = lax.broadcasted_iota(jnp.int32, (tq, grp * tq), 0)
    query = lax.broadcasted_iota(jnp.int32, (tq, grp * tq), 1) % tq
    causal = key_row <= query

    m_ref[...] = jnp.full(m_ref.shape, NEG, F32)
    l_ref[...] = jnp.zeros(l_ref.shape, F32)
    acc_ref[...] = jnp.zeros(acc_ref.shape, F32)

    def block(j, masked):
        k0 = pl.multiple_of(j * tq, tq)
        ck = jnp.concatenate([ckb_ref[g, pl.ds(k0, tq), :] for g in range(grp) for _ in range(tq // LANES)], axis=1)
        s = _dot_nt(kb_ref[pl.ds(k0, tq), :], qs) * scale + (cq - ck)
        if masked:
            s = jnp.where(causal, s, NEG)
        m_old = m_ref[...]
        m_new = jnp.maximum(m_old, jnp.max(s, axis=0, keepdims=True))
        alpha = jnp.exp(m_old - m_new)
        p = jnp.exp(s - m_new)
        l_ref[...] = alpha * l_ref[...] + jnp.sum(p, axis=0, keepdims=True)
        acc_ref[...] = alpha * acc_ref[...] + _dot(vt_ref[:, pl.ds(k0, tq)], p.astype(BF16))
        m_ref[...] = m_new

    def body(j, carry):
        block(j, False)
        return carry

    lax.fori_loop(0, i, body, 0)
    block(i, True)
    ot = acc_ref[...] / l_ref[...]
    for g in range(grp):
        o_ref[:, g * hd:(g + 1) * hd] = ot[:, g * tq:(g + 1) * tq].T.astype(o_ref.dtype)


def _fox_prompt(q, k, v, c, ct, n_seq, seq, rows_out, n_kv, hd):
    tq = _pick_tile(seq, FOX_BLOCK_CAP, LANES)
    nb = seq // tq
    nq = q.shape[1]
    nh = c.shape[1]
    grp = nq // hd // n_kv
    assert hd == LANES and tq % LANES == 0 and seq % tq == 0
    return pl.pallas_call(
        functools.partial(_fox_prompt_kernel, grp=grp, hd=hd),
        grid=(n_seq, n_kv, nb),
        in_specs=[pl.BlockSpec((tq, grp * hd), lambda b, kv, i: (b * nb + i, kv)),
                  pl.BlockSpec((seq, hd), lambda b, kv, i: (b, kv)),
                  pl.BlockSpec((seq, hd), lambda b, kv, i: (b, kv)),
                  pl.BlockSpec((seq, nh), lambda b, kv, i: (b, 0)),
                  pl.BlockSpec((1, 1, grp, seq), lambda b, kv, i: (b, kv, 0, 0))],
        out_specs=pl.BlockSpec((tq, grp * hd), lambda b, kv, i: (b * nb + i, kv)),
        out_shape=jax.ShapeDtypeStruct((rows_out, nq), BF16),
        scratch_shapes=[pltpu.VMEM((seq, hd), BF16), pltpu.VMEM((hd, seq), BF16), pltpu.VMEM((grp, seq, LANES), F32),
                        pltpu.VMEM((1, grp * tq), F32), pltpu.VMEM((1, grp * tq), F32),
                        pltpu.VMEM((hd, grp * tq), F32)],
        compiler_params=_params("parallel", "parallel", "arbitrary"),
        name="fox_prompt",
    )(q, k, v, c, ct)


def _fox_sample_kernel(pt_ref, q_ref, kn_ref, vn_ref, lfn_ref, *refs, n_pg, t_new, nh, scale):
    k_refs, v_refs, lf_refs = refs[:n_pg], refs[n_pg:2 * n_pg], refs[2 * n_pg:3 * n_pg]
    o_ref, m_ref, l_ref, acc_ref, run_ref, cn_ref = refs[3 * n_pg:]
    j = pl.program_id(1)
    pg = kn_ref.shape[1]
    ncols = t_new * nh
    assert 2 * ncols == LANES and pg == LANES
    lane = lax.broadcasted_iota(jnp.int32, (pg, LANES), 1)
    row = lax.broadcasted_iota(jnp.int32, (pg, LANES), 0)
    low = lane < ncols
    lane1 = lax.broadcasted_iota(jnp.int32, (1, LANES), 1)
    diag = row == lane
    qb = q_ref[0]

    def widen(lf_a, lf_b):
        return jnp.concatenate([lf_a] * t_new + [lf_b] * t_new, axis=-1)

    def to_rows(x):
        return jnp.sum(jnp.where(diag, x, 0.0), axis=-1, keepdims=True)

    def accumulate(s_tiles, v_pairs):
        m_old = m_ref[...]
        mx = m_old
        for s in s_tiles:
            mx = jnp.maximum(mx, jnp.max(s, axis=0, keepdims=True))
        m_new = jnp.maximum(mx, pltpu.roll(mx, ncols, 1))
        alpha = jnp.exp(m_old - m_new)
        l_new = alpha * l_ref[...]
        acc = to_rows(alpha)[:ncols] * acc_ref[...]
        for s, (va, vb) in zip(s_tiles, v_pairs):
            p = jnp.exp(s - m_new)
            l_new = l_new + jnp.sum(p, axis=0, keepdims=True)
            pt = p.T.astype(BF16)
            acc = acc + _dot(pt[:ncols], va) + _dot(pt[ncols:], vb)
        m_ref[...] = m_new
        l_ref[...] = l_new
        acc_ref[...] = acc

    @pl.when(j == 0)
    def _():
        lfw = widen(lfn_ref[0], lfn_ref[0])
        cnk = _tri_sums(_tri3(pg, "lower_incl"), lfw)
        tok = (lane % ncols) // nh
        cn = jnp.sum(jnp.where(row == tok, cnk, 0.0), axis=0, keepdims=True)
        cn_ref[...] = cn
        run_ref[...] = jnp.zeros(run_ref.shape, F32)
        m_ref[...] = jnp.full(m_ref.shape, NEG, F32)
        l_ref[...] = jnp.zeros(l_ref.shape, F32)
        acc_ref[...] = jnp.zeros(acc_ref.shape, F32)
        s = _dot(kn_ref[0].astype(BF16), qb) * scale + (cn - cnk)
        s = jnp.where((row <= tok) & low, s, NEG)
        vn = vn_ref[0].astype(BF16)
        accumulate([s], [(vn, vn)])

    tri = _tri3(pg, "upper_excl")
    s_tiles, v_pairs = [], []
    run = run_ref[...]
    for a in range(0, n_pg, 2):
        sa = _dot(k_refs[a][0].astype(BF16), qb)
        sb = _dot(k_refs[a + 1][0].astype(BF16), qb)
        lfw = widen(lf_refs[a][0], lf_refs[a + 1][0])
        tot = jnp.sum(lfw, axis=0, keepdims=True)
        tot_sw = pltpu.roll(tot, ncols, 1)
        bias = _tri_sums(tri, lfw) + run + jnp.where(lane1 >= ncols, tot_sw, 0.0) + cn_ref[...]
        s_tiles.append(jnp.where(low, sa, sb) * scale + bias)
        v_pairs.append((v_refs[a][0].astype(BF16), v_refs[a + 1][0].astype(BF16)))
        run = run + tot + tot_sw
    run_ref[...] = run
    accumulate(s_tiles, v_pairs)

    @pl.when(j == pl.num_programs(1) - 1)
    def _():
        l_tot = l_ref[...] + pltpu.roll(l_ref[...], ncols, 1)
        o_ref[0] = acc_ref[...] / to_rows(l_tot)[:ncols]


def _fox_sample(q, k_new, v_new, lf_new, cache_k, cache_v, cache_lf, page_table, n_kv, hd):
    n, t, nq = q.shape
    nh = nq // hd
    grp = nh // n_kv
    nk = n_kv * hd
    n_pool, pg = cache_k.shape[0] * cache_k.shape[1], cache_k.shape[2]
    n_pages = page_table.shape[1]
    ncols = t * nh
    n_pg = _pick_tile(n_pages, FOX_PAGES_PER_STEP, 2)
    assert n_pages % n_pg == 0 and n_pg % 2 == 0
    qt = q.reshape(n, t, n_kv, grp, hd).transpose(0, 4, 1, 2, 3)
    own = jnp.eye(n_kv, dtype=bool)[None, :, None, None, :, None]
    qb = jnp.where(own, qt[:, None], jnp.zeros((), q.dtype)).reshape(n, nk, ncols)
    qb = jnp.concatenate([qb, qb], axis=-1)
    padrows = lambda a: jnp.pad(a, ((0, 0), (0, pg - t), (0, 0)))
    kc = cache_k.reshape(n_pool, pg, nk)
    vc = cache_v.reshape(n_pool, pg, nk)

    def page(width, p):
        return pl.BlockSpec((1, pg, width), lambda i, j, pt: (pt[i, n_pages - 1 - (j * n_pg + p)], 0, 0))

    per_seq = lambda a, b: pl.BlockSpec((1, a, b), lambda i, j, pt: (i, 0, 0))
    o = pl.pallas_call(
        functools.partial(_fox_sample_kernel, n_pg=n_pg, t_new=t, nh=nh, scale=hd ** -0.5),
        grid_spec=pltpu.PrefetchScalarGridSpec(
            num_scalar_prefetch=1,
            grid=(n, n_pages // n_pg),
            in_specs=[per_seq(nk, 2 * ncols), per_seq(pg, nk), per_seq(pg, nk), per_seq(pg, nh)]
            + [page(nk, p) for p in range(n_pg)] + [page(nk, p) for p in range(n_pg)]
            + [page(nh, p) for p in range(n_pg)],
            out_specs=per_seq(ncols, nk),
            scratch_shapes=[pltpu.VMEM((1, LANES), F32), pltpu.VMEM((1, LANES), F32), pltpu.VMEM((ncols, nk), F32),
                            pltpu.VMEM((1, LANES), F32), pltpu.VMEM((1, LANES), F32)]),
        out_shape=jax.ShapeDtypeStruct((n, ncols, nk), F32),
        compiler_params=_params("parallel", "arbitrary"),
        name="fox_sample",
    )(page_table, qb, padrows(k_new), padrows(v_new), padrows(lf_new), *([kc] * n_pg), *([vc] * n_pg),
      *([cache_lf.reshape(n_pool, pg, nh)] * n_pg))
    o = o.reshape(n, t, n_kv, grp, n_kv, hd)
    return jnp.stack([o[:, :, kv, :, kv, :] for kv in range(n_kv)], axis=2).reshape(n, t, nq)


def _rope_tables(pos, hd):
    half = hd // 2
    inv = ROPE_THETA ** (-2.0 * jnp.arange(half, dtype=F32) / hd)
    ang = pos.astype(F32)[:, None] * inv[None, :]
    cos = jnp.concatenate([jnp.cos(ang), jnp.cos(ang)], axis=-1)
    sin = jnp.concatenate([-jnp.sin(ang), jnp.sin(ang)], axis=-1)
    reps = LANES // hd
    return jnp.tile(cos, (1, reps)), jnp.tile(sin, (1, reps))


def kernel(x_prompt, x_sample, p_prompt, p_sample, cache_conv, cache_swa_k, cache_swa_v, cache_fox_k, cache_fox_v, cache_fox_logf, page_table, norm_mix, norm_ffn, norm_ple, norm_final, conv_w_in, conv_w_dw, conv_b_dw, conv_ln_g, conv_ln_b, conv_w_out, swa_w_qkv, swa_sinks, swa_w_o, fox_w_in, fox_b_f, fox_w_o, ffn_w_gate, ffn_w_up, ffn_w_down, ple_w_proj, ple_w_gate):
    nb, seq, d = x_prompt.shape
    ns, ts, _ = x_sample.shape
    depth = norm_mix.shape[0]
    mp, ms = nb * seq, ns * ts
    m = mp + ms
    tm = _pick_tile(m, ROW_TILE_CAP, BF16_SUBLANES)
    kw = conv_w_dw.shape[1]
    window, swa_kv, swa_hd = cache_swa_k.shape[2:]
    swa_nq, swa_nk = swa_sinks.shape[1] * swa_hd, swa_kv * swa_hd
    fox_kv, fox_hd = cache_fox_k.shape[3:]
    fox_h = fox_b_f.shape[1]
    fox_nq, fox_nk = fox_h * fox_hd, fox_kv * fox_hd
    past_len = page_table.shape[1] * cache_fox_k.shape[2]

    x = jnp.concatenate([x_prompt.reshape(mp, d), x_sample.reshape(ms, d)], axis=0)
    pos = jnp.concatenate([jnp.tile(jnp.arange(seq, dtype=jnp.int32), nb),
                           jnp.tile(past_len + jnp.arange(ts, dtype=jnp.int32), ns)])
    cos, sin = _rope_tables(pos, swa_hd)
    vec = lambda a: a.reshape(1, -1)
    merged = lambda a_p, a_s: lax.dynamic_update_slice(a_p, a_s.reshape(ms, -1), (mp, 0))

    conv_p, conv_s = [], []
    swk_p, swv_p, swk_s, swv_s = [], [], [], []
    fk_p, fv_p, fl_p, fk_s, fv_s, fl_s = [], [], [], [], [], []
    for i in range(depth):
        kind, j = i % 3, i // 3
        g_mix = vec(norm_mix[i])
        if kind == 0:
            u = _conv_in(x, g_mix, conv_w_in[j].astype(BF16), tm)
            dc = u.shape[1]
            conv_w = (conv_w_dw[j], vec(conv_b_dw[j]), vec(conv_ln_g[j]), vec(conv_ln_b[j]))
            y_p, c_p = _conv_core(u, jnp.zeros((nb, kw - 1, dc), F32), *conv_w, nb, seq, m)
            y_s, c_s = _conv_core(u[mp:].reshape(ns, ts, dc), cache_conv[j], *conv_w, ns, ts, None)
            conv_p.append(c_p)
            conv_s.append(c_s)
            mixed = merged(y_p, y_s)
            w_o = conv_w_out[j]
        elif kind == 1:
            q, k, v = _swa_in(x, g_mix, swa_w_qkv[j].astype(BF16), cos, sin, tm, swa_nq, swa_nk, swa_hd)
            o_p = _swa_prompt(q, k, v, swa_sinks[j], nb, seq, m, swa_kv, swa_hd, window)
            o_s, ks_, vs_ = _swa_sample(q[mp:].reshape(ns, ts, swa_nq), k[mp:].reshape(ns, ts, swa_nk),
                                        v[mp:].reshape(ns, ts, swa_nk), cache_swa_k[j].reshape(ns, window, swa_nk),
                                        cache_swa_v[j].reshape(ns, window, swa_nk), swa_sinks[j], swa_kv, swa_hd,
                                        window, past_len)
            k_p, v_p = k[:mp].reshape(nb, seq, swa_kv, swa_hd), v[:mp].reshape(nb, seq, swa_kv, swa_hd)
            swk_p.append(k_p[:, seq - window:])
            swv_p.append(v_p[:, seq - window:])
            swk_s.append(ks_.reshape(ns, window, swa_kv, swa_hd))
            swv_s.append(vs_.reshape(ns, window, swa_kv, swa_hd))
            mixed = merged(o_p, o_s)
            w_o = swa_w_o[j]
        else:
            w_in = fox_w_in[j]
            q, k, v, lf = _fox_in(x, g_mix, w_in[:, :fox_nq + 2 * fox_nk].astype(BF16),
                                  w_in[:, fox_nq + 2 * fox_nk:].astype(BF16), vec(fox_b_f[j]), tm, fox_nq, fox_nk)
            c = _fox_cumsum(lf, nb, seq)
            ct = c.reshape(nb, seq, fox_kv, fox_h // fox_kv).transpose(0, 2, 3, 1)
            o_p = _fox_prompt(q, k, v, c, ct, nb, seq, m, fox_kv, fox_hd)
            o_s = _fox_sample(q[mp:].reshape(ns, ts, fox_nq), k[mp:].reshape(ns, ts, fox_nk),
                              v[mp:].reshape(ns, ts, fox_nk), lf[mp:].reshape(ns, ts, fox_h),
                              cache_fox_k, cache_fox_v, cache_fox_logf, page_table + j * cache_fox_k.shape[1],
                              fox_kv, fox_hd)
            fk_p.append(k[:mp].reshape(nb, seq, fox_kv, fox_hd))
            fv_p.append(v[:mp].reshape(nb, seq, fox_kv, fox_hd))
            fl_p.append(lf[:mp].reshape(nb, seq, fox_h))
            fk_s.append(k[mp:].reshape(ns, ts, fox_kv, fox_hd))
            fv_s.append(v[mp:].reshape(ns, ts, fox_kv, fox_hd))
            fl_s.append(lf[mp:].reshape(ns, ts, fox_h))
            mixed = merged(o_p, o_s.astype(BF16))
            w_o = fox_w_o[j]
        x = _out_proj(mixed, w_o.astype(BF16), x, tm)
        x = _ffn(x, vec(norm_ffn[i]), ffn_w_gate[i].astype(BF16), ffn_w_up[i].astype(BF16),
                 ffn_w_down[i].astype(BF16), tm)
        p = jnp.concatenate([p_prompt[i].reshape(mp, -1), p_sample[i].reshape(ms, -1)], axis=0)
        x = _ple(x, p, vec(norm_ple[i]), ple_w_proj[i].astype(BF16), ple_w_gate[i].astype(BF16), vec(norm_final),
                 tm, final=(i == depth - 1))

    return (x[:mp].reshape(nb, seq, d), x[mp:].reshape(ns, ts, d),
            jnp.stack(conv_p), jnp.stack(conv_s),
            jnp.stack(swk_p), jnp.stack(swv_p), jnp.stack(swk_s), jnp.stack(swv_s),
            jnp.stack(fk_p), jnp.stack(fv_p), jnp.stack(fl_p),
            jnp.stack(fk_s), jnp.stack(fv_s), jnp.stack(fl_s))
```

```python
import functools

import jax
import jax.numpy as jnp
from jax import lax
from jax.experimental import pallas as pl
from jax.experimental.pallas import tpu as pltpu

EPS = 1e-6
NEG = -1e30
ROPE_THETA = 10000.0
F32 = jnp.float32
BF16 = jnp.bfloat16

LANES = 128
SUBLANES = 8
BF16_SUBLANES = 16
VMEM_LIMIT = 56 * 1024 * 1024
ROW_TILE_CAP = 640
FFN_COL_TILE_CAP = 512
COL_CHUNK = 512
ATTN_BLOCK = 128
FOX_BLOCK_CAP = 256
CONV_TIME_TILE_CAP = 128
CONV_ROW_CHUNK = 32
FOX_PAGES_PER_STEP = 16


def _pick_tile(n, cap, mult):
    best = None
    for d in range(mult, min(n, cap) + 1, mult):
        if n % d == 0:
            best = d
    return best if best is not None else n


def _params(*sem):
    return pltpu.CompilerParams(dimension_semantics=sem, vmem_limit_bytes=VMEM_LIMIT)


def _resident(shape):
    nd = len(shape)
    return pl.BlockSpec(shape, lambda *_: (0,) * nd, pipeline_mode=pl.Buffered(1))


def _layer_resident(w, layer):
    return pl.BlockSpec((None,) + w.shape[1:], lambda *_: (layer, 0, 0), pipeline_mode=pl.Buffered(1))


def _rmsnorm(x, g):
    return x * lax.rsqrt(jnp.mean(x * x, axis=-1, keepdims=True) + EPS) * g


def _dot(a, b):
    return jnp.dot(a, b, preferred_element_type=F32)


def _dot_nt(a, b):
    return lax.dot_general(a, b, (((1,), (1,)), ((), ())), preferred_element_type=F32)


def _col_chunks(n, chunk=COL_CHUNK):
    return [(s, min(chunk, n - s)) for s in range(0, n, chunk)]


def _split3(x):
    hi = x.astype(BF16)
    r1 = x - hi.astype(F32)
    mid = r1.astype(BF16)
    lo = (r1 - mid.astype(F32)).astype(BF16)
    return hi, mid, lo


def _conv_in_kernel(x_ref, g_ref, w_ref, u_ref):
    h = _rmsnorm(x_ref[...], g_ref[...]).astype(BF16)
    dc = u_ref.shape[-1]
    for s, n in _col_chunks(dc):
        a = _dot(h, w_ref[:, s:s + n])
        b = _dot(h, w_ref[:, dc + s:dc + s + n])
        u_ref[:, s:s + n] = a * jax.nn.sigmoid(b)


def _conv_in(x, g, w, layer, tm):
    m, d = x.shape
    dc = w.shape[2] // 2
    return pl.pallas_call(
        _conv_in_kernel,
        grid=(m // tm,),
        in_specs=[pl.BlockSpec((tm, d), lambda i: (i, 0)), _resident((1, d)), _layer_resident(w, layer)],
        out_specs=pl.BlockSpec((tm, dc), lambda i: (i, 0)),
        out_shape=jax.ShapeDtypeStruct((m, dc), F32),
        compiler_params=_params("parallel"),
        name="conv_in",
    )(x, g, w)


def _swa_in_kernel(x_ref, g_ref, w_ref, cos_ref, sin_ref, q_ref, k_ref, v_ref, *, hd):
    h = _rmsnorm(x_ref[...], g_ref[...]).astype(BF16)
    cos = cos_ref[...]
    sin = sin_ref[...]
    lane = lax.broadcasted_iota(jnp.int32, cos.shape, 1)
    first_half = (lane % hd) < hd // 2

    def rope(z):
        partner = jnp.where(first_half, pltpu.roll(z, LANES - hd // 2, 1), pltpu.roll(z, hd // 2, 1))
        return z * cos + partner * sin

    nq, nk = q_ref.shape[-1], k_ref.shape[-1]
    for s, n in _col_chunks(nq):
        z = _dot(h, w_ref[:, s:s + n])
        for c in range(0, n, LANES):
            q_ref[:, s + c:s + c + LANES] = rope(z[:, c:c + LANES]).astype(q_ref.dtype)
    for s, n in _col_chunks(nk):
        z = _dot(h, w_ref[:, nq + s:nq + s + n])
        for c in range(0, n, LANES):
            k_ref[:, s + c:s + c + LANES] = rope(z[:, c:c + LANES])
    v_ref[...] = _dot(h, w_ref[:, nq + nk:])


def _swa_in(x, g, w, layer, cos, sin, tm, nq, nk, hd):
    m, d = x.shape
    assert LANES % hd == 0 and nq % LANES == 0 and nk % LANES == 0
    row = lambda n: pl.BlockSpec((tm, n), lambda i: (i, 0))
    return pl.pallas_call(
        functools.partial(_swa_in_kernel, hd=hd),
        grid=(m // tm,),
        in_specs=[row(d), _resident((1, d)), _layer_resident(w, layer), row(LANES), row(LANES)],
        out_specs=[row(nq), row(nk), row(nk)],
        out_shape=[jax.ShapeDtypeStruct((m, nq), BF16), jax.ShapeDtypeStruct((m, nk), F32),
                   jax.ShapeDtypeStruct((m, nk), F32)],
        compiler_params=_params("parallel"),
        name="swa_in",
    )(x, g, w, cos, sin)


def _fox_in_kernel(x_ref, g_ref, w_ref, bf_ref, q_ref, k_ref, v_ref, lf_ref):
    h = _rmsnorm(x_ref[...], g_ref[...]).astype(BF16)
    nq, nk = q_ref.shape[-1], k_ref.shape[-1]
    for s, n in _col_chunks(nq):
        q_ref[:, s:s + n] = _dot(h, w_ref[:, s:s + n]).astype(q_ref.dtype)
    for s, n in _col_chunks(nk):
        k_ref[:, s:s + n] = _dot(h, w_ref[:, nq + s:nq + s + n])
        v_ref[:, s:s + n] = _dot(h, w_ref[:, nq + nk + s:nq + nk + s + n])
    lf_ref[...] = jax.nn.log_sigmoid(_dot(h, w_ref[:, nq + 2 * nk:]) + bf_ref[...])


def _fox_in(x, g, w, layer, bf, tm, nq, nk):
    m, d = x.shape
    nh = w.shape[2] - nq - 2 * nk
    row = lambda n: pl.BlockSpec((tm, n), lambda i: (i, 0))
    return pl.pallas_call(
        _fox_in_kernel,
        grid=(m // tm,),
        in_specs=[row(d), _resident((1, d)), _layer_resident(w, layer), _resident((1, nh))],
        out_specs=[row(nq), row(nk), row(nk), row(nh)],
        out_shape=[jax.ShapeDtypeStruct((m, nq), BF16), jax.ShapeDtypeStruct((m, nk), F32),
                   jax.ShapeDtypeStruct((m, nk), F32), jax.ShapeDtypeStruct((m, nh), F32)],
        compiler_params=_params("parallel"),
        name="fox_in",
    )(x, g, w, bf)


def _out_proj_kernel(ap_ref, as_ref, w_ref, x_ref, o_ref):
    tm = x_ref.shape[0]
    ms = as_ref.shape[0]
    last = pl.num_programs(0) - 1

    def project(a):
        for s, n in _col_chunks(o_ref.shape[-1]):
            o_ref[:, s:s + n] = x_ref[:, s:s + n] + _dot(a, w_ref[:, s:s + n])

    @pl.when(pl.program_id(0) < last)
    def _():
        project(ap_ref[...])

    @pl.when(pl.program_id(0) == last)
    def _():
        project(jnp.concatenate([ap_ref[:tm - ms, :], as_ref[...]], axis=0))


def _out_proj(a_p, a_s, w, layer, x, tm):
    m, d = x.shape
    k = a_p.shape[1]
    ms = a_s.shape[0]
    assert a_p.shape[0] + ms == m and ms <= tm and (tm - ms) % BF16_SUBLANES == 0
    return pl.pallas_call(
        _out_proj_kernel,
        grid=(m // tm,),
        in_specs=[pl.BlockSpec((tm, k), lambda i: (i, 0)), _resident((ms, k)), _layer_resident(w, layer),
                  pl.BlockSpec((tm, d), lambda i: (i, 0))],
        out_specs=pl.BlockSpec((tm, d), lambda i: (i, 0)),
        out_shape=jax.ShapeDtypeStruct((m, d), F32),
        compiler_params=_params("parallel"),
        name="out_proj",
    )(a_p, a_s, w, x)


def _ffn_kernel(x_ref, g_ref, wg_ref, wu_ref, wd_ref, o_ref, h_ref):
    @pl.when(pl.program_id(1) == 0)
    def _():
        x = x_ref[...]
        h_ref[...] = _rmsnorm(x, g_ref[...]).astype(BF16)
        o_ref[...] = x

    h = h_ref[...]
    a = (jax.nn.silu(_dot(h, wg_ref[...])) * _dot(h, wu_ref[...])).astype(BF16)
    o_ref[...] += _dot(a, wd_ref[...])


def _ffn(x, g, wg, wu, wd, layer, tm):
    m, d = x.shape
    f = wg.shape[2]
    tf = _pick_tile(f, FFN_COL_TILE_CAP, LANES)
    return pl.pallas_call(
        _ffn_kernel,
        grid=(m // tm, f // tf),
        in_specs=[pl.BlockSpec((tm, d), lambda i, j: (i, 0)),
                  pl.BlockSpec((1, d), lambda i, j: (0, 0)),
                  pl.BlockSpec((None, d, tf), lambda i, j: (layer, 0, j)),
                  pl.BlockSpec((None, d, tf), lambda i, j: (layer, 0, j)),
                  pl.BlockSpec((None, tf, d), lambda i, j: (layer, j, 0))],
        out_specs=pl.BlockSpec((tm, d), lambda i, j: (i, 0)),
        out_shape=jax.ShapeDtypeStruct((m, d), F32),
        scratch_shapes=[pltpu.VMEM((tm, d), BF16)],
        compiler_params=_params("parallel", "arbitrary"),
        name="ffn",
    )(x, g, wg, wu, wd)


def _ple_kernel(x_ref, pp_ref, ps_ref, g_ref, wp_ref, wg_ref, gf_ref, o_ref, p_ref, *, final):
    tm = x_ref.shape[0]
    ms = ps_ref.shape[0]
    last = pl.num_programs(0) - 1

    @pl.when(pl.program_id(0) < last)
    def _():
        p_ref[...] = pp_ref[...].astype(BF16)

    @pl.when(pl.program_id(0) == last)
    def _():
        p_ref[:tm - ms, :] = pp_ref[:tm - ms, :].astype(BF16)
        p_ref[tm - ms:, :] = ps_ref[...].astype(BF16)

    h = _rmsnorm(x_ref[...], g_ref[...]).astype(BF16)
    p = p_ref[...]
    for s, n in _col_chunks(o_ref.shape[-1]):
        gate = jax.nn.sigmoid(_dot(h, wg_ref[:, s:s + n]))
        o_ref[:, s:s + n] = x_ref[:, s:s + n] + _dot(p, wp_ref[:, s:s + n]) * gate
    if final:
        o_ref[...] = _rmsnorm(o_ref[...], gf_ref[...])


def _ple(x, p_p, p_s, g, wp, wg, layer, gf, tm, final):
    m, d = x.shape
    dp = p_p.shape[2]
    ms = p_s.shape[1]
    assert p_p.shape[1] + ms == m and ms <= tm and (tm - ms) % BF16_SUBLANES == 0
    return pl.pallas_call(
        functools.partial(_ple_kernel, final=final),
        grid=(m // tm,),
        in_specs=[pl.BlockSpec((tm, d), lambda i: (i, 0)), pl.BlockSpec((None, tm, dp), lambda i: (layer, i, 0)),
                  _layer_resident(p_s, layer), _resident((1, d)), _layer_resident(wp, layer),
                  _layer_resident(wg, layer), _resident((1, d))],
        out_specs=pl.BlockSpec((tm, d), lambda i: (i, 0)),
        out_shape=jax.ShapeDtypeStruct((m, d), F32),
        scratch_shapes=[pltpu.VMEM((tm, dp), BF16)],
        compiler_params=_params("parallel"),
        name="ple_final" if final else "ple",
    )(x, p_p, p_s, g, wp, wg, gf)


def _conv_kernel(u_ref, hist_ref, w_ref, b_ref, lg_ref, lb_ref, y_ref, cache_ref, ext_ref, sh_ref, wb_ref, conv_ref,
                 *, tt, kw, pad):
    t = pl.program_id(1)
    off = pad - (kw - 1)

    sub = wb_ref.shape[1]

    @pl.when(t == 0)
    def _():
        ext_ref[off:pad, :] = hist_ref[0]
        for k in range(kw):
            wb_ref[k] = jnp.broadcast_to(w_ref[k:k + 1, :], wb_ref.shape[1:])

    @pl.when(t > 0)
    def _():
        ext_ref[off:pad, :] = ext_ref[tt + off:tt + pad, :]

    ext_ref[pad:pad + tt, :] = u_ref[...].reshape(tt, u_ref.shape[-1])
    cache_ref[0] = ext_ref[tt + off:tt + pad, :]

    d = ext_ref.shape[-1]
    rc = min(CONV_ROW_CHUNK, tt)
    rows, cn = sh_ref.shape[1:]
    for cs in range(0, d, cn):
        for b in range(1, SUBLANES):
            sh_ref[b - 1] = ext_ref[b:b + rows, cs:cs + cn]
        for r in range(0, tt, rc):
            accs = [jnp.broadcast_to(b_ref[:, cs:cs + cn], (sub, cn))] * (rc // sub)
            for k in range(kw):
                a, b = divmod(off + k, SUBLANES)
                wk = wb_ref[k, :, cs:cs + cn]
                for q in range(rc // sub):
                    r0 = a * SUBLANES + r + q * sub
                    src = ext_ref[r0:r0 + sub, cs:cs + cn] if b == 0 else sh_ref[b - 1, r0:r0 + sub, :]
                    accs[q] = accs[q] + src * wk
            for q in range(rc // sub):
                conv_ref[r + q * sub:r + (q + 1) * sub, cs:cs + cn] = accs[q]
    y = conv_ref[...]
    mu = jnp.mean(y, axis=-1, keepdims=True)
    yc = y - mu
    var = jnp.mean(yc * yc, axis=-1, keepdims=True)
    z = yc * lax.rsqrt(var + EPS) * lg_ref[...] + lb_ref[...]
    y_ref[...] = jax.nn.silu(z).astype(y_ref.dtype).reshape(y_ref.shape)


def _conv_core(u, hist, w_dw, b_dw, ln_g, ln_b, n, t, rows_out):
    d = u.shape[-1]
    kw = w_dw.shape[0]
    pad = -(-(kw - 1) // SUBLANES) * SUBLANES
    if u.ndim == 2:
        tt = _pick_tile(t, CONV_TIME_TILE_CAP, SUBLANES)
        nt = t // tt
        row_spec = pl.BlockSpec((tt, d), lambda i, j: (i * nt + j, 0))
        y_shape = (rows_out, d)
    else:
        tt = t
        row_spec = pl.BlockSpec((1, t, d), lambda i, j: (i, 0, 0))
        y_shape = (n, t, d)
    assert tt == t or tt >= kw - 1
    assert d % COL_CHUNK == 0
    vec = lambda: pl.BlockSpec((1, d), lambda i, j: (0, 0))
    return pl.pallas_call(
        functools.partial(_conv_kernel, tt=tt, kw=kw, pad=pad),
        grid=(n, t // tt),
        in_specs=[row_spec, pl.BlockSpec((1, kw - 1, d), lambda i, j: (i, 0, 0)),
                  pl.BlockSpec((kw, d), lambda i, j: (0, 0)), vec(), vec(), vec()],
        out_specs=[row_spec, pl.BlockSpec((1, kw - 1, d), lambda i, j: (i, 0, 0))],
        out_shape=[jax.ShapeDtypeStruct(y_shape, BF16), jax.ShapeDtypeStruct((n, kw - 1, d), F32)],
        scratch_shapes=[pltpu.VMEM((pad + tt, d), F32),
                        pltpu.VMEM((SUBLANES - 1, tt + pad - SUBLANES, COL_CHUNK), F32),
                        pltpu.VMEM((kw, min(SUBLANES, tt), d), F32), pltpu.VMEM((tt, d), F32)],
        compiler_params=_params("parallel", "arbitrary"),
        name="conv_core",
    )(u, hist, w_dw, b_dw, ln_g, ln_b)


def _swa_prompt_kernel(sink_ref, q_ref, kp_ref, kc_ref, vp_ref, vc_ref, o_ref, *, n_heads, n_kv, hd, window):
    i = pl.program_id(1)
    tq = q_ref.shape[0]
    grp = n_heads // n_kv
    r = lax.broadcasted_iota(jnp.int32, (tq, 2 * tq), 0)
    c = lax.broadcasted_iota(jnp.int32, (tq, 2 * tq), 1)
    dist = r + tq - c
    valid = (dist >= 0) & (dist <= window) & ((c >= tq) | (i > 0))
    scale = hd ** -0.5
    for kv in range(n_kv):
        ks = slice(kv * hd, (kv + 1) * hd)
        kcat = jnp.concatenate([kp_ref[:, ks], kc_ref[:, ks]], axis=0).astype(BF16)
        vcat = jnp.concatenate([vp_ref[:, ks], vc_ref[:, ks]], axis=0).astype(BF16)
        for g in range(grp):
            h = kv * grp + g
            s = _dot_nt(q_ref[:, h * hd:(h + 1) * hd], kcat) * scale
            s = jnp.where(valid, s, NEG)
            sink = sink_ref[h]
            m = jnp.maximum(jnp.max(s, axis=-1, keepdims=True), sink)
            e = jnp.exp(s - m)
            den = jnp.sum(e, axis=-1, keepdims=True) + jnp.exp(sink - m)
            p = (e / den).astype(BF16)
            o_ref[:, h * hd:(h + 1) * hd] = _dot(p, vcat).astype(o_ref.dtype)


def _swa_prompt(q, k, v, sinks, n_seq, seq, rows_out, n_kv, hd, window):
    tq = ATTN_BLOCK
    assert seq % tq == 0 and window <= tq
    nb = seq // tq
    nq, nk = q.shape[1], k.shape[1]
    cur = lambda n: pl.BlockSpec((tq, n), lambda b, i: (b * nb + i, 0))
    prev = lambda n: pl.BlockSpec((tq, n), lambda b, i: (jnp.maximum(b * nb + i - 1, 0), 0))
    return pl.pallas_call(
        functools.partial(_swa_prompt_kernel, n_heads=nq // hd, n_kv=n_kv, hd=hd, window=window),
        grid=(n_seq, nb),
        in_specs=[pl.BlockSpec(memory_space=pltpu.SMEM), cur(nq), prev(nk), cur(nk), prev(nk), cur(nk)],
        out_specs=cur(nq),
        out_shape=jax.ShapeDtypeStruct((rows_out, nq), BF16),
        compiler_params=_params("parallel", "arbitrary"),
        name="swa_prompt",
    )(sinks, q, k, k, v, v)


def _swa_sample_kernel(q_ref, sink_ref, kc_ref, kn_ref, vc_ref, vn_ref, o_ref, ko_ref, vo_ref, *, hd, window, t_new,
                       past_len):
    nseq, n_kv, rows, _ = q_ref.shape
    nbuf = kc_ref.shape[1]
    ts = nbuf + t_new
    r = lax.broadcasted_iota(jnp.int32, (rows, ts), 0)
    c = lax.broadcasted_iota(jnp.int32, (rows, ts), 1)
    dist = (r % t_new) + nbuf - c
    valid = (dist >= 0) & (dist <= window) & (c >= nbuf - past_len)
    scale = hd ** -0.5
    for n in range(nseq):
        kall = jnp.concatenate([kc_ref[n], kn_ref[n]], axis=0)
        vall = jnp.concatenate([vc_ref[n], vn_ref[n]], axis=0)
        ko_ref[n] = kall[ts - nbuf:]
        vo_ref[n] = vall[ts - nbuf:]
        kb = kall.astype(BF16)
        vb = vall.astype(BF16)
        for kv in range(n_kv):
            s = _dot_nt(q_ref[n, kv], kb[:, kv * hd:(kv + 1) * hd]) * scale
            s = jnp.where(valid, s, NEG)
            sink = sink_ref[kv]
            m = jnp.maximum(jnp.max(s, axis=-1, keepdims=True), sink)
            e = jnp.exp(s - m)
            den = jnp.sum(e, axis=-1, keepdims=True) + jnp.exp(sink - m)
            p = (e / den).astype(BF16)
            o_ref[n, kv] = _dot(p, vb[:, kv * hd:(kv + 1) * hd]).astype(o_ref.dtype)


def _swa_sample(q, k_new, v_new, k_cache, v_cache, sinks, n_kv, hd, window, past_len):
    n, t, nq = q.shape
    grp = nq // hd // n_kv
    nbuf = k_cache.shape[1]
    nk = n_kv * hd
    rows = grp * t
    sb = _pick_tile(n, 8, 1)
    qr = q.reshape(n, t, n_kv, grp, hd).transpose(0, 2, 3, 1, 4).reshape(n, n_kv, rows, hd)
    sink_rows = jnp.broadcast_to(sinks.reshape(n_kv, grp, 1, 1), (n_kv, grp, t, 1)).reshape(n_kv, rows, 1)
    seq3 = lambda a, b: pl.BlockSpec((sb, a, b), lambda i: (i, 0, 0))
    seq4 = pl.BlockSpec((sb, n_kv, rows, hd), lambda i: (i, 0, 0, 0))
    o, ko, vo = pl.pallas_call(
        functools.partial(_swa_sample_kernel, hd=hd, window=window, t_new=t, past_len=past_len),
        grid=(n // sb,),
        in_specs=[seq4, _resident((n_kv, rows, 1)), seq3(nbuf, nk), seq3(t, nk), seq3(nbuf, nk), seq3(t, nk)],
        out_specs=[seq4, seq3(nbuf, nk), seq3(nbuf, nk)],
        out_shape=[jax.ShapeDtypeStruct((n, n_kv, rows, hd), BF16), jax.ShapeDtypeStruct((n, nbuf, nk), F32),
                   jax.ShapeDtypeStruct((n, nbuf, nk), F32)],
        compiler_params=_params("parallel"),
        name="swa_sample",
    )(qr, sink_rows, k_cache, k_new, v_cache, v_new)
    o = o.reshape(n, n_kv, grp, t, hd).transpose(0, 3, 1, 2, 4).reshape(n, t, nq)
    return o, ko, vo


def _tri3(n, kind):
    r = lax.broadcasted_iota(jnp.int32, (n, 3 * n), 0)
    c = lax.broadcasted_iota(jnp.int32, (n, 3 * n), 1) % n
    keep = (c <= r) if kind == "lower_incl" else (c > r)
    return jnp.where(keep, 1.0, 0.0).astype(BF16)


def _tri_sums(tri3, x):
    return _dot(tri3, jnp.concatenate(_split3(x), axis=0))


def _fox_cumsum_kernel(lf_ref, c_ref, *, blk):
    s, nh = lf_ref.shape
    tri = _tri3(blk, "lower_incl")

    def body(j, carry):
        r0 = pl.multiple_of(j * blk, blk)
        cs = _tri_sums(tri, lf_ref[pl.ds(r0, blk), :]) + carry
        c_ref[pl.ds(r0, blk), :] = cs
        return cs[blk - 1:blk, :]

    lax.fori_loop(0, s // blk, body, jnp.zeros((1, nh), F32))


def _fox_cumsum(lf, n_seq, seq):
    nh = lf.shape[1]
    blk = ATTN_BLOCK
    assert seq % blk == 0
    return pl.pallas_call(
        functools.partial(_fox_cumsum_kernel, blk=blk),
        grid=(n_seq,),
        in_specs=[pl.BlockSpec((seq, nh), lambda b: (b, 0))],
        out_specs=pl.BlockSpec((seq, nh), lambda b: (b, 0)),
        out_shape=jax.ShapeDtypeStruct((n_seq * seq, nh), F32),
        compiler_params=_params("parallel"),
        name="fox_cumsum",
    )(lf)


def _fox_prompt_kernel(q_ref, k_ref, v_ref, c_ref, ct_ref, o_ref, kb_ref, vt_ref, ckb_ref, m_ref, l_ref, acc_ref, *,
                       grp, hd):
    kv = pl.program_id(1)
    i = pl.program_id(2)
    tq = q_ref.shape[0]
    seq, nh = c_ref.shape
    scale = hd ** -0.5

    @pl.when(i == 0)
    def _():
        kb_ref[...] = k_ref[...].astype(BF16)
        for b in range(0, seq, LANES):
            vt_ref[:, b:b + LANES] = v_ref[b:b + LANES, :].T.astype(BF16)
        parts = _split3(c_ref[...])
        head = lax.broadcasted_iota(jnp.int32, (nh, LANES), 0)
        for g in range(grp):
            sel = jnp.where(head == kv * grp + g, 1.0, 0.0).astype(BF16)
            ckb_ref[g] = _dot(parts[0], sel) + _dot(parts[1], sel) + _dot(parts[2], sel)

    q0 = pl.multiple_of(i * tq, tq)
    qs = jnp.concatenate([q_ref[:, g * hd:(g + 1) * hd] for g in range(grp)], axis=0)
    cq = jnp.concatenate([ct_ref[0, 0, g:g + 1, pl.ds(q0, tq)] for g in range(grp)], axis=1)
    key_row = lax.broadcasted_iota(jnp.int32, (tq, grp * tq), 0)
    query = lax.broadcasted_iota(jnp.int32, (tq, grp * tq), 1) % tq
    causal = key_row <= query

    m_ref[...] = jnp.full(m_ref.shape, NEG, F32)
    l_ref[...] = jnp.zeros(l_ref.shape, F32)
    acc_ref[...] = jnp.zeros(acc_ref.shape, F32)

    def block(j, masked):
        k0 = pl.multiple_of(j * tq, tq)
        ck = jnp.concatenate([ckb_ref[g, pl.ds(k0, tq), :] for g in range(grp) for _ in range(tq // LANES)], axis=1)
        s = _dot_nt(kb_ref[pl.ds(k0, tq), :], qs) * scale + (cq - ck)
        if masked:
            s = jnp.where(causal, s, NEG)
        m_old = m_ref[...]
        m_new = jnp.maximum(m_old, jnp.max(s, axis=0, keepdims=True))
        alpha = jnp.exp(m_old - m_new)
        p = jnp.exp(s - m_new)
        l_ref[...] = alpha * l_ref[...] + jnp.sum(p, axis=0, keepdims=True)
        acc_ref[...] = alpha * acc_ref[...] + _dot(vt_ref[:, pl.ds(k0, tq)], p.astype(BF16))
        m_ref[...] = m_new

    def body(j, carry):
        block(j, False)
        return carry

    lax.fori_loop(0, i, body, 0)
    block(i, True)
    ot = acc_ref[...] / l_ref[...]
    for g in range(grp):
        o_ref[:, g * hd:(g + 1) * hd] = ot[:, g * tq:(g + 1) * tq].T.astype(o_ref.dtype)


def _fox_prompt(q, k, v, c, ct, n_seq, seq, rows_out, n_kv, hd):
    tq = _pick_tile(seq, FOX_BLOCK_CAP, LANES)
    nb = seq // tq
    nq = q.shape[1]
    nh = c.shape[1]
    grp = nq // hd // n_kv
    assert hd == LANES and tq % LANES == 0 and seq % tq == 0
    return pl.pallas_call(
        functools.partial(_fox_prompt_kernel, grp=grp, hd=hd),
        grid=(n_seq, n_kv, nb),
        in_specs=[pl.BlockSpec((tq, grp * hd), lambda b, kv, i: (b * nb + i, kv)),
                  pl.BlockSpec((seq, hd), lambda b, kv, i: (b, kv)),
                  pl.BlockSpec((seq, hd), lambda b, kv, i: (b, kv)),
                  pl.BlockSpec((seq, nh), lambda b, kv, i: (b, 0)),
                  pl.BlockSpec((1, 1, grp, seq), lambda b, kv, i: (b, kv, 0, 0))],
        out_specs=pl.BlockSpec((tq, grp * hd), lambda b, kv, i: (b * nb + i, kv)),
        out_shape=jax.ShapeDtypeStruct((rows_out, nq), BF16),
        scratch_shapes=[pltpu.VMEM((seq, hd), BF16), pltpu.VMEM((hd, seq), BF16), pltpu.VMEM((grp, seq, LANES), F32),
                        pltpu.VMEM((1, grp * tq), F32), pltpu.VMEM((1, grp * tq), F32),
                        pltpu.VMEM((hd, grp * tq), F32)],
        compiler_params=_params("parallel", "parallel", "arbitrary"),
        name="fox_prompt",
    )(q, k, v, c, ct)


def _fox_sample_kernel(pt_ref, q_ref, kn_ref, vn_ref, lfn_ref, *refs, n_pg, t_new, nh, scale):
    k_refs, v_refs, lf_refs = refs[:n_pg], refs[n_pg:2 * n_pg], refs[2 * n_pg:3 * n_pg]
    o_ref, m_ref, l_ref, acc_ref, run_ref, cn_ref = refs[3 * n_pg:]
    j = pl.program_id(1)
    pg = kn_ref.shape[1]
    ncols = t_new * nh
    assert 2 * ncols == LANES and pg == LANES
    lane = lax.broadcasted_iota(jnp.int32, (pg, LANES), 1)
    row = lax.broadcasted_iota(jnp.int32, (pg, LANES), 0)
    low = lane < ncols
    lane1 = lax.broadcasted_iota(jnp.int32, (1, LANES), 1)
    diag = row == lane
    qb = q_ref[0]

    def widen(lf_a, lf_b):
        return jnp.concatenate([lf_a] * t_new + [lf_b] * t_new, axis=-1)

    def to_rows(x):
        return jnp.sum(jnp.where(diag, x, 0.0), axis=-1, keepdims=True)

    def accumulate(s_tiles, v_pairs):
        m_old = m_ref[...]
        mx = m_old
        for s in s_tiles:
            mx = jnp.maximum(mx, jnp.max(s, axis=0, keepdims=True))
        m_new = jnp.maximum(mx, pltpu.roll(mx, ncols, 1))
        alpha = jnp.exp(m_old - m_new)
        l_new = alpha * l_ref[...]
        acc = to_rows(alpha)[:ncols] * acc_ref[...]
        for s, (va, vb) in zip(s_tiles, v_pairs):
            p = jnp.exp(s - m_new)
            l_new = l_new + jnp.sum(p, axis=0, keepdims=True)
            pt = p.T.astype(BF16)
            acc = acc + _dot(pt[:ncols], va) + _dot(pt[ncols:], vb)
        m_ref[...] = m_new
        l_ref[...] = l_new
        acc_ref[...] = acc

    @pl.when(j == 0)
    def _():
        lfw = widen(lfn_ref[0], lfn_ref[0])
        cnk = _tri_sums(_tri3(pg, "lower_incl"), lfw)
        tok = (lane % ncols) // nh
        cn = jnp.sum(jnp.where(row == tok, cnk, 0.0), axis=0, keepdims=True)
        cn_ref[...] = cn
        run_ref[...] = jnp.zeros(run_ref.shape, F32)
        m_ref[...] = jnp.full(m_ref.shape, NEG, F32)
        l_ref[...] = jnp.zeros(l_ref.shape, F32)
        acc_ref[...] = jnp.zeros(acc_ref.shape, F32)
        s = _dot(kn_ref[0].astype(BF16), qb) * scale + (cn - cnk)
        s = jnp.where((row <= tok) & low, s, NEG)
        vn = vn_ref[0].astype(BF16)
        accumulate([s], [(vn, vn)])

    tri = _tri3(pg, "upper_excl")
    s_tiles, v_pairs = [], []
    run = run_ref[...]
    n_kv = qb.shape[0] // k_refs[0].shape[1]

    def page_rows(ref):
        return jnp.concatenate([ref[pl.ds(kv, pg, stride=n_kv), :] for kv in range(n_kv)], axis=-1).astype(BF16)

    for a in range(0, n_pg, 2):
        sa = _dot(page_rows(k_refs[a]), qb)
        sb = _dot(page_rows(k_refs[a + 1]), qb)
        lfw = widen(lf_refs[a][0], lf_refs[a + 1][0])
        tot = jnp.sum(lfw, axis=0, keepdims=True)
        tot_sw = pltpu.roll(tot, ncols, 1)
        bias = _tri_sums(tri, lfw) + run + jnp.where(lane1 >= ncols, tot_sw, 0.0) + cn_ref[...]
        s_tiles.append(jnp.where(low, sa, sb) * scale + bias)
        v_pairs.append((page_rows(v_refs[a]), page_rows(v_refs[a + 1])))
        run = run + tot + tot_sw
    run_ref[...] = run
    accumulate(s_tiles, v_pairs)

    @pl.when(j == pl.num_programs(1) - 1)
    def _():
        l_tot = l_ref[...] + pltpu.roll(l_ref[...], ncols, 1)
        o_ref[0] = acc_ref[...] / to_rows(l_tot)[:ncols]


def _fox_sample(q, k_new, v_new, lf_new, cache_k, cache_v, cache_lf, layer, page_table, n_kv, hd):
    n, t, nq = q.shape
    nh = nq // hd
    grp = nh // n_kv
    nk = n_kv * hd
    pg = cache_k.shape[2]
    n_pages = page_table.shape[1]
    ncols = t * nh
    n_pg = _pick_tile(n_pages, FOX_PAGES_PER_STEP, 2)
    assert n_pages % n_pg == 0 and n_pg % 2 == 0
    qt = q.reshape(n, t, n_kv, grp, hd).transpose(0, 4, 1, 2, 3)
    own = jnp.eye(n_kv, dtype=bool)[None, :, None, None, :, None]
    qb = jnp.where(own, qt[:, None], jnp.zeros((), q.dtype)).reshape(n, nk, ncols)
    qb = jnp.concatenate([qb, qb], axis=-1)
    padrows = lambda a: jnp.pad(a, ((0, 0), (0, pg - t), (0, 0)))

    rows2d = lambda c: c.reshape(c.shape[0], c.shape[1], pg * n_kv, hd)

    def page_id(i, j, pt, p):
        return pt[i, n_pages - 1 - (j * n_pg + p)]

    def kv_page(p):
        return pl.BlockSpec((None, None, pg * n_kv, hd), lambda i, j, pt: (layer, page_id(i, j, pt, p), 0, 0))

    def lf_page(p):
        return pl.BlockSpec((None, 1, pg, nh), lambda i, j, pt: (layer, page_id(i, j, pt, p), 0, 0))

    per_seq = lambda a, b: pl.BlockSpec((1, a, b), lambda i, j, pt: (i, 0, 0))
    o = pl.pallas_call(
        functools.partial(_fox_sample_kernel, n_pg=n_pg, t_new=t, nh=nh, scale=hd ** -0.5),
        grid_spec=pltpu.PrefetchScalarGridSpec(
            num_scalar_prefetch=1,
            grid=(n, n_pages // n_pg),
            in_specs=[per_seq(nk, 2 * ncols), per_seq(pg, nk), per_seq(pg, nk), per_seq(pg, nh)]
            + [kv_page(p) for p in range(n_pg)] + [kv_page(p) for p in range(n_pg)]
            + [lf_page(p) for p in range(n_pg)],
            out_specs=per_seq(ncols, nk),
            scratch_shapes=[pltpu.VMEM((1, LANES), F32), pltpu.VMEM((1, LANES), F32), pltpu.VMEM((ncols, nk), F32),
                            pltpu.VMEM((1, LANES), F32), pltpu.VMEM((1, LANES), F32)]),
        out_shape=jax.ShapeDtypeStruct((n, ncols, nk), F32),
        compiler_params=_params("parallel", "arbitrary"),
        name="fox_sample",
    )(page_table, qb, padrows(k_new), padrows(v_new), padrows(lf_new), *([rows2d(cache_k)] * n_pg),
      *([rows2d(cache_v)] * n_pg),
      *([cache_lf] * n_pg))
    o = o.reshape(n, t, n_kv, grp, n_kv, hd)
    return jnp.stack([o[:, :, kv, :, kv, :] for kv in range(n_kv)], axis=2).reshape(n, t, nq)


def _rope_tables(pos, hd):
    half = hd // 2
    inv = ROPE_THETA ** (-2.0 * jnp.arange(half, dtype=F32) / hd)
    ang = pos.astype(F32)[:, None] * inv[None, :]
    cos = jnp.concatenate([jnp.cos(ang), jnp.cos(ang)], axis=-1)
    sin = jnp.concatenate([-jnp.sin(ang), jnp.sin(ang)], axis=-1)
    reps = LANES // hd
    return jnp.tile(cos, (1, reps)), jnp.tile(sin, (1, reps))


def kernel(x_prompt, x_sample, p_prompt, p_sample, cache_conv, cache_swa_k, cache_swa_v, cache_fox_k, cache_fox_v, cache_fox_logf, page_table, norm_mix, norm_ffn, norm_ple, norm_final, conv_w_in, conv_w_dw, conv_b_dw, conv_ln_g, conv_ln_b, conv_w_out, swa_w_qkv, swa_sinks, swa_w_o, fox_w_in, fox_b_f, fox_w_o, ffn_w_gate, ffn_w_up, ffn_w_down, ple_w_proj, ple_w_gate):
    nb, seq, d = x_prompt.shape
    ns, ts, _ = x_sample.shape
    depth = norm_mix.shape[0]
    mp, ms = nb * seq, ns * ts
    m = mp + ms
    tm = _pick_tile(m, ROW_TILE_CAP, BF16_SUBLANES)
    kw = conv_w_dw.shape[1]
    window, swa_kv, swa_hd = cache_swa_k.shape[2:]
    swa_nq, swa_nk = swa_sinks.shape[1] * swa_hd, swa_kv * swa_hd
    fox_kv, fox_hd = cache_fox_k.shape[3:]
    fox_h = fox_b_f.shape[1]
    fox_nq, fox_nk = fox_h * fox_hd, fox_kv * fox_hd
    past_len = page_table.shape[1] * cache_fox_k.shape[2]

    x = jnp.concatenate([x_prompt.reshape(mp, d), x_sample.reshape(ms, d)], axis=0)
    pos = jnp.concatenate([jnp.tile(jnp.arange(seq, dtype=jnp.int32), nb),
                           jnp.tile(past_len + jnp.arange(ts, dtype=jnp.int32), ns)])
    cos, sin = _rope_tables(pos, swa_hd)
    vec = lambda a: a.reshape(1, -1)
    bf16 = lambda w: w.astype(BF16)
    conv_w_in, conv_w_out, swa_w_qkv, swa_w_o, fox_w_in, fox_w_o = map(
        bf16, (conv_w_in, conv_w_out, swa_w_qkv, swa_w_o, fox_w_in, fox_w_o))
    ffn_w_gate, ffn_w_up, ffn_w_down, ple_w_proj, ple_w_gate = map(
        bf16, (ffn_w_gate, ffn_w_up, ffn_w_down, ple_w_proj, ple_w_gate))
    p_p = p_prompt.reshape(depth, mp, -1)
    p_s = p_sample.reshape(depth, ms, -1)

    conv_p, conv_s = [], []
    swk_p, swv_p, swk_s, swv_s = [], [], [], []
    fk_p, fv_p, fl_p, fk_s, fv_s, fl_s = [], [], [], [], [], []
    for i in range(depth):
        kind, j = i % 3, i // 3
        g_mix = vec(norm_mix[i])
        if kind == 0:
            u = _conv_in(x, g_mix, conv_w_in, j, tm)
            dc = u.shape[1]
            conv_w = (conv_w_dw[j], vec(conv_b_dw[j]), vec(conv_ln_g[j]), vec(conv_ln_b[j]))
            o_p, c_p = _conv_core(u, jnp.zeros((nb, kw - 1, dc), F32), *conv_w, nb, seq, mp)
            o_s, c_s = _conv_core(u[mp:].reshape(ns, ts, dc), cache_conv[j], *conv_w, ns, ts, None)
            conv_p.append(c_p)
            conv_s.append(c_s)
            w_o = conv_w_out
        elif kind == 1:
            q, k, v = _swa_in(x, g_mix, swa_w_qkv, j, cos, sin, tm, swa_nq, swa_nk, swa_hd)
            o_p = _swa_prompt(q, k, v, swa_sinks[j], nb, seq, mp, swa_kv, swa_hd, window)
            o_s, ks_, vs_ = _swa_sample(q[mp:].reshape(ns, ts, swa_nq), k[mp:].reshape(ns, ts, swa_nk),
                                        v[mp:].reshape(ns, ts, swa_nk), cache_swa_k[j].reshape(ns, window, swa_nk),
                                        cache_swa_v[j].reshape(ns, window, swa_nk), swa_sinks[j], swa_kv, swa_hd,
                                        window, past_len)
            k_p, v_p = k[:mp].reshape(nb, seq, swa_kv, swa_hd), v[:mp].reshape(nb, seq, swa_kv, swa_hd)
            swk_p.append(k_p[:, seq - window:])
            swv_p.append(v_p[:, seq - window:])
            swk_s.append(ks_.reshape(ns, window, swa_kv, swa_hd))
            swv_s.append(vs_.reshape(ns, window, swa_kv, swa_hd))
            w_o = swa_w_o
        else:
            q, k, v, lf = _fox_in(x, g_mix, fox_w_in, j, vec(fox_b_f[j]), tm, fox_nq, fox_nk)
            c = _fox_cumsum(lf, nb, seq)
            ct = c.reshape(nb, seq, fox_kv, fox_h // fox_kv).transpose(0, 2, 3, 1)
            o_p = _fox_prompt(q, k, v, c, ct, nb, seq, mp, fox_kv, fox_hd)
            o_s = _fox_sample(q[mp:].reshape(ns, ts, fox_nq), k[mp:].reshape(ns, ts, fox_nk),
                              v[mp:].reshape(ns, ts, fox_nk), lf[mp:].reshape(ns, ts, fox_h),
                              cache_fox_k, cache_fox_v, cache_fox_logf, j, page_table, fox_kv, fox_hd)
            fk_p.append(k[:mp].reshape(nb, seq, fox_kv, fox_hd))
            fv_p.append(v[:mp].reshape(nb, seq, fox_kv, fox_hd))
            fl_p.append(lf[:mp].reshape(nb, seq, fox_h))
            fk_s.append(k[mp:].reshape(ns, ts, fox_kv, fox_hd))
            fv_s.append(v[mp:].reshape(ns, ts, fox_kv, fox_hd))
            fl_s.append(lf[mp:].reshape(ns, ts, fox_h))
            w_o = fox_w_o
        x = _out_proj(o_p, o_s.reshape(ms, -1).astype(BF16), w_o, j, x, tm)
        x = _ffn(x, vec(norm_ffn[i]), ffn_w_gate, ffn_w_up, ffn_w_down, i, tm)
        x = _ple(x, p_p, p_s, vec(norm_ple[i]), ple_w_proj, ple_w_gate, i, vec(norm_final), tm,
                 final=(i == depth - 1))

    return (x[:mp].reshape(nb, seq, d), x[mp:].reshape(ns, ts, d),
            jnp.stack(conv_p), jnp.stack(conv_s),
            jnp.stack(swk_p), jnp.stack(swv_p), jnp.stack(swk_s), jnp.stack(swv_s),
            jnp.stack(fk_p), jnp.stack(fv_p), jnp.stack(fl_p),
            jnp.stack(fk_s), jnp.stack(fv_s), jnp.stack(fl_s))
```

```python
import functools

import jax
import jax.numpy as jnp
from jax import lax
from jax.experimental import pallas as pl
from jax.experimental.pallas import tpu as pltpu

EPS = 1e-6
NEG = -1e30
ROPE_THETA = 10000.0
F32 = jnp.float32
BF16 = jnp.bfloat16

LANES = 128
SUBLANES = 8
BF16_SUBLANES = 16
VMEM_LIMIT = 56 * 1024 * 1024
ROW_TILE_CAP = 640
FFN_COL_TILE_CAP = 512
COL_CHUNK = 512
ATTN_BLOCK = 128
FOX_BLOCK_CAP = 256
CONV_TIME_TILE_CAP = 128
CONV_ROW_CHUNK = 32
FOX_PAGES_PER_STEP = 16


def _pick_tile(n, cap, mult):
    best = None
    for d in range(mult, min(n, cap) + 1, mult):
        if n % d == 0:
            best = d
    return best if best is not None else n


def _params(*sem):
    return pltpu.CompilerParams(dimension_semantics=sem, vmem_limit_bytes=VMEM_LIMIT)


def _resident(shape):
    nd = len(shape)
    return pl.BlockSpec(shape, lambda *_: (0,) * nd, pipeline_mode=pl.Buffered(1))


def _layer_resident(w, layer):
    return pl.BlockSpec((None,) + w.shape[1:], lambda *_: (layer, 0, 0), pipeline_mode=pl.Buffered(1))


def _rmsnorm(x, g):
    return x * lax.rsqrt(jnp.mean(x * x, axis=-1, keepdims=True) + EPS) * g


def _dot(a, b):
    return jnp.dot(a, b, preferred_element_type=F32)


def _dot_nt(a, b):
    return lax.dot_general(a, b, (((1,), (1,)), ((), ())), preferred_element_type=F32)


def _col_chunks(n, chunk=COL_CHUNK):
    return [(s, min(chunk, n - s)) for s in range(0, n, chunk)]


def _split3(x):
    hi = x.astype(BF16)
    r1 = x - hi.astype(F32)
    mid = r1.astype(BF16)
    lo = (r1 - mid.astype(F32)).astype(BF16)
    return hi, mid, lo


def _conv_in_kernel(x_ref, g_ref, w_ref, u_ref):
    h = _rmsnorm(x_ref[...], g_ref[...]).astype(BF16)
    dc = u_ref.shape[-1]
    for s, n in _col_chunks(dc):
        a = _dot(h, w_ref[:, s:s + n])
        b = _dot(h, w_ref[:, dc + s:dc + s + n])
        u_ref[:, s:s + n] = a * jax.nn.sigmoid(b)


def _conv_in(x, g, w, layer, tm):
    m, d = x.shape
    dc = w.shape[2] // 2
    return pl.pallas_call(
        _conv_in_kernel,
        grid=(m // tm,),
        in_specs=[pl.BlockSpec((tm, d), lambda i: (i, 0)), _resident((1, d)), _layer_resident(w, layer)],
        out_specs=pl.BlockSpec((tm, dc), lambda i: (i, 0)),
        out_shape=jax.ShapeDtypeStruct((m, dc), F32),
        compiler_params=_params("parallel"),
        name="conv_in",
    )(x, g, w)


def _swa_in_kernel(x_ref, g_ref, w_ref, cos_ref, sin_ref, q_ref, k_ref, v_ref, *, hd):
    h = _rmsnorm(x_ref[...], g_ref[...]).astype(BF16)
    cos = cos_ref[...]
    sin = sin_ref[...]
    lane = lax.broadcasted_iota(jnp.int32, cos.shape, 1)
    first_half = (lane % hd) < hd // 2

    def rope(z):
        partner = jnp.where(first_half, pltpu.roll(z, LANES - hd // 2, 1), pltpu.roll(z, hd // 2, 1))
        return z * cos + partner * sin

    nq, nk = q_ref.shape[-1], k_ref.shape[-1]
    for s, n in _col_chunks(nq):
        z = _dot(h, w_ref[:, s:s + n])
        for c in range(0, n, LANES):
            q_ref[:, s + c:s + c + LANES] = rope(z[:, c:c + LANES]).astype(q_ref.dtype)
    for s, n in _col_chunks(nk):
        z = _dot(h, w_ref[:, nq + s:nq + s + n])
        for c in range(0, n, LANES):
            k_ref[:, s + c:s + c + LANES] = rope(z[:, c:c + LANES])
    v_ref[...] = _dot(h, w_ref[:, nq + nk:])


def _swa_in(x, g, w, layer, cos, sin, tm, nq, nk, hd):
    m, d = x.shape
    assert LANES % hd == 0 and nq % LANES == 0 and nk % LANES == 0
    row = lambda n: pl.BlockSpec((tm, n), lambda i: (i, 0))
    return pl.pallas_call(
        functools.partial(_swa_in_kernel, hd=hd),
        grid=(m // tm,),
        in_specs=[row(d), _resident((1, d)), _layer_resident(w, layer), row(LANES), row(LANES)],
        out_specs=[row(nq), row(nk), row(nk)],
        out_shape=[jax.ShapeDtypeStruct((m, nq), BF16), jax.ShapeDtypeStruct((m, nk), F32),
                   jax.ShapeDtypeStruct((m, nk), F32)],
        compiler_params=_params("parallel"),
        name="swa_in",
    )(x, g, w, cos, sin)


def _fox_in_kernel(x_ref, g_ref, w_ref, bf_ref, q_ref, k_ref, v_ref, lf_ref):
    h = _rmsnorm(x_ref[...], g_ref[...]).astype(BF16)
    nq, nk = q_ref.shape[-1], k_ref.shape[-1]
    for s, n in _col_chunks(nq):
        q_ref[:, s:s + n] = _dot(h, w_ref[:, s:s + n]).astype(q_ref.dtype)
    for s, n in _col_chunks(nk):
        k_ref[:, s:s + n] = _dot(h, w_ref[:, nq + s:nq + s + n])
        v_ref[:, s:s + n] = _dot(h, w_ref[:, nq + nk + s:nq + nk + s + n])
    lf_ref[...] = jax.nn.log_sigmoid(_dot(h, w_ref[:, nq + 2 * nk:]) + bf_ref[...])


def _fox_in(x, g, w, layer, bf, tm, nq, nk):
    m, d = x.shape
    nh = w.shape[2] - nq - 2 * nk
    row = lambda n: pl.BlockSpec((tm, n), lambda i: (i, 0))
    return pl.pallas_call(
        _fox_in_kernel,
        grid=(m // tm,),
        in_specs=[row(d), _resident((1, d)), _layer_resident(w, layer), _resident((1, nh))],
        out_specs=[row(nq), row(nk), row(nk), row(nh)],
        out_shape=[jax.ShapeDtypeStruct((m, nq), BF16), jax.ShapeDtypeStruct((m, nk), F32),
                   jax.ShapeDtypeStruct((m, nk), F32), jax.ShapeDtypeStruct((m, nh), F32)],
        compiler_params=_params("parallel"),
        name="fox_in",
    )(x, g, w, bf)


def _out_proj_kernel(ap_ref, as_ref, w_ref, x_ref, o_ref):
    tm = x_ref.shape[0]
    ms = as_ref.shape[0]
    last = pl.num_programs(0) - 1

    def project(a):
        for s, n in _col_chunks(o_ref.shape[-1]):
            o_ref[:, s:s + n] = x_ref[:, s:s + n] + _dot(a, w_ref[:, s:s + n])

    @pl.when(pl.program_id(0) < last)
    def _():
        project(ap_ref[...])

    @pl.when(pl.program_id(0) == last)
    def _():
        project(jnp.concatenate([ap_ref[:tm - ms, :], as_ref[...]], axis=0))


def _out_proj(a_p, a_s, w, layer, x, tm):
    m, d = x.shape
    k = a_p.shape[1]
    ms = a_s.shape[0]
    assert a_p.shape[0] + ms == m and ms <= tm and (tm - ms) % BF16_SUBLANES == 0
    return pl.pallas_call(
        _out_proj_kernel,
        grid=(m // tm,),
        in_specs=[pl.BlockSpec((tm, k), lambda i: (i, 0)), _resident((ms, k)), _layer_resident(w, layer),
                  pl.BlockSpec((tm, d), lambda i: (i, 0))],
        out_specs=pl.BlockSpec((tm, d), lambda i: (i, 0)),
        out_shape=jax.ShapeDtypeStruct((m, d), F32),
        compiler_params=_params("parallel"),
        name="out_proj",
    )(a_p, a_s, w, x)


def _ffn_kernel(x_ref, g_ref, wg_ref, wu_ref, wd_ref, o_ref, h_ref):
    @pl.when(pl.program_id(1) == 0)
    def _():
        x = x_ref[...]
        h_ref[...] = _rmsnorm(x, g_ref[...]).astype(BF16)
        o_ref[...] = x

    h = h_ref[...]
    a = (jax.nn.silu(_dot(h, wg_ref[...])) * _dot(h, wu_ref[...])).astype(BF16)
    o_ref[...] += _dot(a, wd_ref[...])


def _ffn(x, g, wg, wu, wd, layer, tm):
    m, d = x.shape
    f = wg.shape[2]
    tf = _pick_tile(f, FFN_COL_TILE_CAP, LANES)
    return pl.pallas_call(
        _ffn_kernel,
        grid=(m // tm, f // tf),
        in_specs=[pl.BlockSpec((tm, d), lambda i, j: (i, 0)),
                  pl.BlockSpec((1, d), lambda i, j: (0, 0)),
                  pl.BlockSpec((None, d, tf), lambda i, j: (layer, 0, j)),
                  pl.BlockSpec((None, d, tf), lambda i, j: (layer, 0, j)),
                  pl.BlockSpec((None, tf, d), lambda i, j: (layer, j, 0))],
        out_specs=pl.BlockSpec((tm, d), lambda i, j: (i, 0)),
        out_shape=jax.ShapeDtypeStruct((m, d), F32),
        scratch_shapes=[pltpu.VMEM((tm, d), BF16)],
        compiler_params=_params("parallel", "arbitrary"),
        name="ffn",
    )(x, g, wg, wu, wd)


def _ple_kernel(x_ref, pp_ref, ps_ref, g_ref, wp_ref, wg_ref, gf_ref, o_ref, p_ref, *, final):
    tm = x_ref.shape[0]
    ms = ps_ref.shape[0]
    last = pl.num_programs(0) - 1

    @pl.when(pl.program_id(0) < last)
    def _():
        p_ref[...] = pp_ref[...].astype(BF16)

    @pl.when(pl.program_id(0) == last)
    def _():
        p_ref[:tm - ms, :] = pp_ref[:tm - ms, :].astype(BF16)
        p_ref[tm - ms:, :] = ps_ref[...].astype(BF16)

    h = _rmsnorm(x_ref[...], g_ref[...]).astype(BF16)
    p = p_ref[...]
    for s, n in _col_chunks(o_ref.shape[-1]):
        gate = jax.nn.sigmoid(_dot(h, wg_ref[:, s:s + n]))
        o_ref[:, s:s + n] = x_ref[:, s:s + n] + _dot(p, wp_ref[:, s:s + n]) * gate
    if final:
        o_ref[...] = _rmsnorm(o_ref[...], gf_ref[...])


def _ple(x, p_p, p_s, g, wp, wg, layer, gf, tm, final):
    m, d = x.shape
    dp = p_p.shape[2]
    ms = p_s.shape[1]
    assert p_p.shape[1] + ms == m and ms <= tm and (tm - ms) % BF16_SUBLANES == 0
    return pl.pallas_call(
        functools.partial(_ple_kernel, final=final),
        grid=(m // tm,),
        in_specs=[pl.BlockSpec((tm, d), lambda i: (i, 0)), pl.BlockSpec((None, tm, dp), lambda i: (layer, i, 0)),
                  _layer_resident(p_s, layer), _resident((1, d)), _layer_resident(wp, layer),
                  _layer_resident(wg, layer), _resident((1, d))],
        out_specs=pl.BlockSpec((tm, d), lambda i: (i, 0)),
        out_shape=jax.ShapeDtypeStruct((m, d), F32),
        scratch_shapes=[pltpu.VMEM((tm, dp), BF16)],
        compiler_params=_params("parallel"),
        name="ple_final" if final else "ple",
    )(x, p_p, p_s, g, wp, wg, gf)


def _conv_kernel(u_ref, hist_ref, w_ref, b_ref, lg_ref, lb_ref, y_ref, cache_ref, ext_ref, sh_ref, wb_ref, conv_ref,
                 *, tt, kw, pad):
    t = pl.program_id(1)
    off = pad - (kw - 1)

    sub = wb_ref.shape[1]

    @pl.when(t == 0)
    def _():
        ext_ref[off:pad, :] = hist_ref[0]
        for k in range(kw):
            wb_ref[k] = jnp.broadcast_to(w_ref[k:k + 1, :], wb_ref.shape[1:])

    @pl.when(t > 0)
    def _():
        ext_ref[off:pad, :] = ext_ref[tt + off:tt + pad, :]

    ext_ref[pad:pad + tt, :] = u_ref[...].reshape(tt, u_ref.shape[-1])
    cache_ref[0] = ext_ref[tt + off:tt + pad, :]

    d = ext_ref.shape[-1]
    rc = min(CONV_ROW_CHUNK, tt)
    rows, cn = sh_ref.shape[1:]
    for cs in range(0, d, cn):
        for b in range(1, SUBLANES):
            sh_ref[b - 1] = ext_ref[b:b + rows, cs:cs + cn]
        for r in range(0, tt, rc):
            accs = [jnp.broadcast_to(b_ref[:, cs:cs + cn], (sub, cn))] * (rc // sub)
            for k in range(kw):
                a, b = divmod(off + k, SUBLANES)
                wk = wb_ref[k, :, cs:cs + cn]
                for q in range(rc // sub):
                    r0 = a * SUBLANES + r + q * sub
                    src = ext_ref[r0:r0 + sub, cs:cs + cn] if b == 0 else sh_ref[b - 1, r0:r0 + sub, :]
                    accs[q] = accs[q] + src * wk
            for q in range(rc // sub):
                conv_ref[r + q * sub:r + (q + 1) * sub, cs:cs + cn] = accs[q]
    y = conv_ref[...]
    mu = jnp.mean(y, axis=-1, keepdims=True)
    yc = y - mu
    var = jnp.mean(yc * yc, axis=-1, keepdims=True)
    z = yc * lax.rsqrt(var + EPS) * lg_ref[...] + lb_ref[...]
    y_ref[...] = jax.nn.silu(z).astype(y_ref.dtype).reshape(y_ref.shape)


def _conv_core(u, hist, w_dw, b_dw, ln_g, ln_b, n, t, rows_out):
    d = u.shape[-1]
    kw = w_dw.shape[0]
    pad = -(-(kw - 1) // SUBLANES) * SUBLANES
    if u.ndim == 2:
        tt = _pick_tile(t, CONV_TIME_TILE_CAP, SUBLANES)
        nt = t // tt
        row_spec = pl.BlockSpec((tt, d), lambda i, j: (i * nt + j, 0))
        y_shape = (rows_out, d)
    else:
        tt = t
        row_spec = pl.BlockSpec((1, t, d), lambda i, j: (i, 0, 0))
        y_shape = (n, t, d)
    assert tt == t or tt >= kw - 1
    assert d % COL_CHUNK == 0
    vec = lambda: pl.BlockSpec((1, d), lambda i, j: (0, 0))
    return pl.pallas_call(
        functools.partial(_conv_kernel, tt=tt, kw=kw, pad=pad),
        grid=(n, t // tt),
        in_specs=[row_spec, pl.BlockSpec((1, kw - 1, d), lambda i, j: (i, 0, 0)),
                  pl.BlockSpec((kw, d), lambda i, j: (0, 0)), vec(), vec(), vec()],
        out_specs=[row_spec, pl.BlockSpec((1, kw - 1, d), lambda i, j: (i, 0, 0))],
        out_shape=[jax.ShapeDtypeStruct(y_shape, BF16), jax.ShapeDtypeStruct((n, kw - 1, d), F32)],
        scratch_shapes=[pltpu.VMEM((pad + tt, d), F32),
                        pltpu.VMEM((SUBLANES - 1, tt + pad - SUBLANES, COL_CHUNK), F32),
                        pltpu.VMEM((kw, min(SUBLANES, tt), d), F32), pltpu.VMEM((tt, d), F32)],
        compiler_params=_params("parallel", "arbitrary"),
        name="conv_core",
    )(u, hist, w_dw, b_dw, ln_g, ln_b)


def _swa_prompt_kernel(sink_ref, q_ref, kp_ref, kc_ref, vp_ref, vc_ref, o_ref, *, n_heads, n_kv, hd, window):
    i = pl.program_id(1)
    tq = q_ref.shape[0]
    grp = n_heads // n_kv
    key = lax.broadcasted_iota(jnp.int32, (2 * tq, grp * tq), 0)
    qry = lax.broadcasted_iota(jnp.int32, (2 * tq, grp * tq), 1) % tq
    dist = qry + tq - key
    valid = (dist >= 0) & (dist <= window) & ((key >= tq) | (i > 0))
    scale = hd ** -0.5
    kcat = jnp.concatenate([kp_ref[...], kc_ref[...]], axis=0).astype(BF16)
    vt = jnp.concatenate([vp_ref[...], vc_ref[...]], axis=0).T.astype(BF16)
    per_tile = LANES // hd
    for kv in range(n_kv):
        heads = range(kv * grp, (kv + 1) * grp)
        qs = jnp.concatenate([q_ref[:, h * hd:(h + 1) * hd] for h in heads], axis=0)
        sink = jnp.concatenate([jnp.full((1, tq), sink_ref[h], F32) for h in heads], axis=1)
        s = _dot_nt(kcat[:, kv * hd:(kv + 1) * hd], qs) * scale
        s = jnp.where(valid, s, NEG)
        m = jnp.maximum(jnp.max(s, axis=0, keepdims=True), sink)
        e = jnp.exp(s - m)
        den = jnp.sum(e, axis=0, keepdims=True) + jnp.exp(sink - m)
        ot = _dot(vt[kv * hd:(kv + 1) * hd, :], e.astype(BF16)) / den
        for g in range(0, grp, per_tile):
            tile = jnp.concatenate([ot[:, (g + t) * tq:(g + t + 1) * tq] for t in range(per_tile)], axis=0)
            c0 = (kv * grp + g) * hd
            o_ref[:, c0:c0 + LANES] = tile.T.astype(o_ref.dtype)


def _swa_prompt(q, k, v, sinks, n_seq, seq, rows_out, n_kv, hd, window):
    tq = ATTN_BLOCK
    assert seq % tq == 0 and window <= tq
    nb = seq // tq
    nq, nk = q.shape[1], k.shape[1]
    cur = lambda n: pl.BlockSpec((tq, n), lambda b, i: (b * nb + i, 0))
    prev = lambda n: pl.BlockSpec((tq, n), lambda b, i: (jnp.maximum(b * nb + i - 1, 0), 0))
    return pl.pallas_call(
        functools.partial(_swa_prompt_kernel, n_heads=nq // hd, n_kv=n_kv, hd=hd, window=window),
        grid=(n_seq, nb),
        in_specs=[pl.BlockSpec(memory_space=pltpu.SMEM), cur(nq), prev(nk), cur(nk), prev(nk), cur(nk)],
        out_specs=cur(nq),
        out_shape=jax.ShapeDtypeStruct((rows_out, nq), BF16),
        compiler_params=_params("parallel", "arbitrary"),
        name="swa_prompt",
    )(sinks, q, k, k, v, v)


def _swa_sample_kernel(q_ref, sink_ref, kc_ref, kn_ref, vc_ref, vn_ref, o_ref, ko_ref, vo_ref, *, hd, window, t_new,
                       past_len):
    nseq, n_kv, rows, _ = q_ref.shape
    nbuf = kc_ref.shape[1]
    ts = nbuf + t_new
    r = lax.broadcasted_iota(jnp.int32, (rows, ts), 0)
    c = lax.broadcasted_iota(jnp.int32, (rows, ts), 1)
    dist = (r % t_new) + nbuf - c
    valid = (dist >= 0) & (dist <= window) & (c >= nbuf - past_len)
    scale = hd ** -0.5
    for n in range(nseq):
        kall = jnp.concatenate([kc_ref[n], kn_ref[n]], axis=0)
        vall = jnp.concatenate([vc_ref[n], vn_ref[n]], axis=0)
        ko_ref[n] = kall[ts - nbuf:]
        vo_ref[n] = vall[ts - nbuf:]
        kb = kall.astype(BF16)
        vb = vall.astype(BF16)
        for kv in range(n_kv):
            s = _dot_nt(q_ref[n, kv], kb[:, kv * hd:(kv + 1) * hd]) * scale
            s = jnp.where(valid, s, NEG)
            sink = sink_ref[kv]
            m = jnp.maximum(jnp.max(s, axis=-1, keepdims=True), sink)
            e = jnp.exp(s - m)
            den = jnp.sum(e, axis=-1, keepdims=True) + jnp.exp(sink - m)
            p = (e / den).astype(BF16)
            o_ref[n, kv] = _dot(p, vb[:, kv * hd:(kv + 1) * hd]).astype(o_ref.dtype)


def _swa_sample(q, k_new, v_new, k_cache, v_cache, sinks, n_kv, hd, window, past_len):
    n, t, nq = q.shape
    grp = nq // hd // n_kv
    nbuf = k_cache.shape[1]
    nk = n_kv * hd
    rows = grp * t
    sb = _pick_tile(n, 8, 1)
    qr = q.reshape(n, t, n_kv, grp, hd).transpose(0, 2, 3, 1, 4).reshape(n, n_kv, rows, hd)
    sink_rows = jnp.broadcast_to(sinks.reshape(n_kv, grp, 1, 1), (n_kv, grp, t, 1)).reshape(n_kv, rows, 1)
    seq3 = lambda a, b: pl.BlockSpec((sb, a, b), lambda i: (i, 0, 0))
    seq4 = pl.BlockSpec((sb, n_kv, rows, hd), lambda i: (i, 0, 0, 0))
    o, ko, vo = pl.pallas_call(
        functools.partial(_swa_sample_kernel, hd=hd, window=window, t_new=t, past_len=past_len),
        grid=(n // sb,),
        in_specs=[seq4, _resident((n_kv, rows, 1)), seq3(nbuf, nk), seq3(t, nk), seq3(nbuf, nk), seq3(t, nk)],
        out_specs=[seq4, seq3(nbuf, nk), seq3(nbuf, nk)],
        out_shape=[jax.ShapeDtypeStruct((n, n_kv, rows, hd), BF16), jax.ShapeDtypeStruct((n, nbuf, nk), F32),
                   jax.ShapeDtypeStruct((n, nbuf, nk), F32)],
        compiler_params=_params("parallel"),
        name="swa_sample",
    )(qr, sink_rows, k_cache, k_new, v_cache, v_new)
    o = o.reshape(n, n_kv, grp, t, hd).transpose(0, 3, 1, 2, 4).reshape(n, t, nq)
    return o, ko, vo


def _tri3(n, kind):
    r = lax.broadcasted_iota(jnp.int32, (n, 3 * n), 0)
    c = lax.broadcasted_iota(jnp.int32, (n, 3 * n), 1) % n
    keep = (c <= r) if kind == "lower_incl" else (c > r)
    return jnp.where(keep, 1.0, 0.0).astype(BF16)


def _tri_sums(tri3, x):
    return _dot(tri3, jnp.concatenate(_split3(x), axis=0))


def _fox_cumsum_kernel(lf_ref, c_ref, *, blk):
    s, nh = lf_ref.shape
    tri = _tri3(blk, "lower_incl")

    def body(j, carry):
        r0 = pl.multiple_of(j * blk, blk)
        cs = _tri_sums(tri, lf_ref[pl.ds(r0, blk), :]) + carry
        c_ref[pl.ds(r0, blk), :] = cs
        return cs[blk - 1:blk, :]

    lax.fori_loop(0, s // blk, body, jnp.zeros((1, nh), F32))


def _fox_cumsum(lf, n_seq, seq):
    nh = lf.shape[1]
    blk = ATTN_BLOCK
    assert seq % blk == 0
    return pl.pallas_call(
        functools.partial(_fox_cumsum_kernel, blk=blk),
        grid=(n_seq,),
        in_specs=[pl.BlockSpec((seq, nh), lambda b: (b, 0))],
        out_specs=pl.BlockSpec((seq, nh), lambda b: (b, 0)),
        out_shape=jax.ShapeDtypeStruct((n_seq * seq, nh), F32),
        compiler_params=_params("parallel"),
        name="fox_cumsum",
    )(lf)


def _fox_prompt_kernel(q_ref, k_ref, v_ref, c_ref, ct_ref, o_ref, kb_ref, vt_ref, ckb_ref, m_ref, l_ref, acc_ref, *,
                       grp, hd):
    kv = pl.program_id(1)
    i = pl.program_id(2)
    tq = q_ref.shape[0]
    seq, nh = c_ref.shape
    scale = hd ** -0.5

    @pl.when(i == 0)
    def _():
        kb_ref[...] = k_ref[...].astype(BF16)
        for b in range(0, seq, LANES):
            vt_ref[:, b:b + LANES] = v_ref[b:b + LANES, :].T.astype(BF16)
        parts = _split3(c_ref[...])
        head = lax.broadcasted_iota(jnp.int32, (nh, LANES), 0)
        for g in range(grp):
            sel = jnp.where(head == kv * grp + g, 1.0, 0.0).astype(BF16)
            ckb_ref[g] = _dot(parts[0], sel) + _dot(parts[1], sel) + _dot(parts[2], sel)

    q0 = pl.multiple_of(i * tq, tq)
    qs = jnp.concatenate([q_ref[:, g * hd:(g + 1) * hd] for g in range(grp)], axis=0)
    cq = jnp.concatenate([ct_ref[0, 0, g:g + 1, pl.ds(q0, tq)] for g in range(grp)], axis=1)
    key_row = lax.broadcasted_iota(jnp.int32, (tq, grp * tq), 0)
    query = lax.broadcasted_iota(jnp.int32, (tq, grp * tq), 1) % tq
    causal = key_row <= query

    m_ref[...] = jnp.full(m_ref.shape, NEG, F32)
    l_ref[...] = jnp.zeros(l_ref.shape, F32)
    acc_ref[...] = jnp.zeros(acc_ref.shape, F32)

    def block(j, masked):
        k0 = pl.multiple_of(j * tq, tq)
        ck = jnp.concatenate([ckb_ref[g, pl.ds(k0, tq), :] for g in range(grp) for _ in range(tq // LANES)], axis=1)
        s = _dot_nt(kb_ref[pl.ds(k0, tq), :], qs) * scale + (cq - ck)
        if masked:
            s = jnp.where(causal, s, NEG)
        m_old = m_ref[...]
        m_new = jnp.maximum(m_old, jnp.max(s, axis=0, keepdims=True))
        alpha = jnp.exp(m_old - m_new)
        p = jnp.exp(s - m_new)
        l_ref[...] = alpha * l_ref[...] + jnp.sum(p, axis=0, keepdims=True)
        acc_ref[...] = alpha * acc_ref[...] + _dot(vt_ref[:, pl.ds(k0, tq)], p.astype(BF16))
        m_ref[...] = m_new

    def body(j, carry):
        block(j, False)
        return carry

    lax.fori_loop(0, i, body, 0)
    block(i, True)
    ot = acc_ref[...] / l_ref[...]
    for g in range(grp):
        o_ref[:, g * hd:(g + 1) * hd] = ot[:, g * tq:(g + 1) * tq].T.astype(o_ref.dtype)


def _fox_prompt(q, k, v, c, ct, n_seq, seq, rows_out, n_kv, hd):
    tq = _pick_tile(seq, FOX_BLOCK_CAP, LANES)
    nb = seq // tq
    nq = q.shape[1]
    nh = c.shape[1]
    grp = nq // hd // n_kv
    assert hd == LANES and tq % LANES == 0 and seq % tq == 0
    return pl.pallas_call(
        functools.partial(_fox_prompt_kernel, grp=grp, hd=hd),
        grid=(n_seq, n_kv, nb),
        in_specs=[pl.BlockSpec((tq, grp * hd), lambda b, kv, i: (b * nb + i, kv)),
                  pl.BlockSpec((seq, hd), lambda b, kv, i: (b, kv)),
                  pl.BlockSpec((seq, hd), lambda b, kv, i: (b, kv)),
                  pl.BlockSpec((seq, nh), lambda b, kv, i: (b, 0)),
                  pl.BlockSpec((1, 1, grp, seq), lambda b, kv, i: (b, kv, 0, 0))],
        out_specs=pl.BlockSpec((tq, grp * hd), lambda b, kv, i: (b * nb + i, kv)),
        out_shape=jax.ShapeDtypeStruct((rows_out, nq), BF16),
        scratch_shapes=[pltpu.VMEM((seq, hd), BF16), pltpu.VMEM((hd, seq), BF16), pltpu.VMEM((grp, seq, LANES), F32),
                        pltpu.VMEM((1, grp * tq), F32), pltpu.VMEM((1, grp * tq), F32),
                        pltpu.VMEM((hd, grp * tq), F32)],
        compiler_params=_params("parallel", "parallel", "arbitrary"),
        name="fox_prompt",
    )(q, k, v, c, ct)


def _fox_sample_kernel(pt_ref, q_ref, kn_ref, vn_ref, lfn_ref, *refs, n_pg, t_new, nh, scale):
    k_refs, v_refs, lf_refs = refs[:n_pg], refs[n_pg:2 * n_pg], refs[2 * n_pg:3 * n_pg]
    o_ref, m_ref, l_ref, acc_ref, run_ref, cn_ref = refs[3 * n_pg:]
    j = pl.program_id(1)
    pg = kn_ref.shape[1]
    ncols = t_new * nh
    assert 2 * ncols == LANES and pg == LANES
    lane = lax.broadcasted_iota(jnp.int32, (pg, LANES), 1)
    row = lax.broadcasted_iota(jnp.int32, (pg, LANES), 0)
    low = lane < ncols
    lane1 = lax.broadcasted_iota(jnp.int32, (1, LANES), 1)
    diag = row == lane
    qb = q_ref[0]

    def widen(lf_a, lf_b):
        return jnp.concatenate([lf_a] * t_new + [lf_b] * t_new, axis=-1)

    def to_rows(x):
        return jnp.sum(jnp.where(diag, x, 0.0), axis=-1, keepdims=True)

    def accumulate(s_tiles, v_pairs):
        m_old = m_ref[...]
        mx = m_old
        for s in s_tiles:
            mx = jnp.maximum(mx, jnp.max(s, axis=0, keepdims=True))
        m_new = jnp.maximum(mx, pltpu.roll(mx, ncols, 1))
        alpha = jnp.exp(m_old - m_new)
        l_new = alpha * l_ref[...]
        acc = to_rows(alpha)[:ncols] * acc_ref[...]
        for s, (va, vb) in zip(s_tiles, v_pairs):
            p = jnp.exp(s - m_new)
            l_new = l_new + jnp.sum(p, axis=0, keepdims=True)
            pt = p.T.astype(BF16)
            acc = acc + _dot(pt[:ncols], va) + _dot(pt[ncols:], vb)
        m_ref[...] = m_new
        l_ref[...] = l_new
        acc_ref[...] = acc

    @pl.when(j == 0)
    def _():
        lfw = widen(lfn_ref[0], lfn_ref[0])
        cnk = _tri_sums(_tri3(pg, "lower_incl"), lfw)
        tok = (lane % ncols) // nh
        cn = jnp.sum(jnp.where(row == tok, cnk, 0.0), axis=0, keepdims=True)
        cn_ref[...] = cn
        run_ref[...] = jnp.zeros(run_ref.shape, F32)
        m_ref[...] = jnp.full(m_ref.shape, NEG, F32)
        l_ref[...] = jnp.zeros(l_ref.shape, F32)
        acc_ref[...] = jnp.zeros(acc_ref.shape, F32)
        s = _dot(kn_ref[0].astype(BF16), qb) * scale + (cn - cnk)
        s = jnp.where((row <= tok) & low, s, NEG)
        vn = vn_ref[0].astype(BF16)
        accumulate([s], [(vn, vn)])

    k3 = lax.broadcasted_iota(jnp.int32, (3 * pg, pg), 0) % pg
    later3 = jnp.where(k3 > lax.broadcasted_iota(jnp.int32, (3 * pg, pg), 1), 1.0, 0.0).astype(BF16)
    col3 = lax.broadcasted_iota(jnp.int32, (LANES, 6 * nh), 0)
    r3 = lax.broadcasted_iota(jnp.int32, (LANES, 6 * nh), 1) % (2 * nh)
    spread3 = jnp.where(r3 == (col3 // ncols) * nh + col3 % nh, 1.0, 0.0).astype(BF16)
    colt = lax.broadcasted_iota(jnp.int32, (2 * nh, LANES), 1)
    spread_t = lax.broadcasted_iota(jnp.int32, (2 * nh, LANES), 0) == (colt // ncols) * nh + colt % nh
    s_tiles, v_pairs = [], []
    run = run_ref[...]
    n_kv = qb.shape[0] // k_refs[0].shape[1]

    def page_rows(ref):
        return jnp.concatenate([ref[pl.ds(kv, pg, stride=n_kv), :] for kv in range(n_kv)], axis=-1).astype(BF16)

    lft = jnp.concatenate([ref[...] for ref in lf_refs], axis=0)
    after_all = _dot(jnp.concatenate(_split3(lft), axis=1), later3)
    page_tot = jnp.sum(lft, axis=1, keepdims=True)
    qk = [_dot(page_rows(ref), qb) for ref in k_refs]
    for a in range(0, n_pg, 2):
        sa, sb = qk[a], qk[a + 1]
        after = after_all[a * nh:(a + 2) * nh]
        after_cols = _dot(spread3, jnp.concatenate(_split3(after), axis=0)).T
        tot = jnp.sum(jnp.where(spread_t, page_tot[a * nh:(a + 2) * nh], 0.0), axis=0, keepdims=True)
        tot_sw = pltpu.roll(tot, ncols, 1)
        bias = after_cols + run + jnp.where(lane1 >= ncols, tot_sw, 0.0) + cn_ref[...]
        s_tiles.append(jnp.where(low, sa, sb) * scale + bias)
        v_pairs.append((page_rows(v_refs[a]), page_rows(v_refs[a + 1])))
        run = run + tot + tot_sw
    run_ref[...] = run
    accumulate(s_tiles, v_pairs)

    @pl.when(j == pl.num_programs(1) - 1)
    def _():
        l_tot = l_ref[...] + pltpu.roll(l_ref[...], ncols, 1)
        o_ref[0] = acc_ref[...] / to_rows(l_tot)[:ncols]


def _fox_sample(q, k_new, v_new, lf_new, cache_k, cache_v, cache_lf, layer, page_table, n_kv, hd):
    n, t, nq = q.shape
    nh = nq // hd
    grp = nh // n_kv
    nk = n_kv * hd
    pg = cache_k.shape[2]
    n_pages = page_table.shape[1]
    ncols = t * nh
    n_pg = _pick_tile(n_pages, FOX_PAGES_PER_STEP, 2)
    assert n_pages % n_pg == 0 and n_pg % 2 == 0
    qt = q.reshape(n, t, n_kv, grp, hd).transpose(0, 4, 1, 2, 3)
    own = jnp.eye(n_kv, dtype=bool)[None, :, None, None, :, None]
    qb = jnp.where(own, qt[:, None], jnp.zeros((), q.dtype)).reshape(n, nk, ncols)
    qb = jnp.concatenate([qb, qb], axis=-1)
    padrows = lambda a: jnp.pad(a, ((0, 0), (0, pg - t), (0, 0)))

    rows2d = lambda c: c.reshape(c.shape[0], c.shape[1], pg * n_kv, hd)

    def page_id(i, j, pt, p):
        return pt[i, n_pages - 1 - (j * n_pg + p)]

    def kv_page(p):
        return pl.BlockSpec((None, None, pg * n_kv, hd), lambda i, j, pt: (layer, page_id(i, j, pt, p), 0, 0))

    def lf_page(p):
        return pl.BlockSpec((None, None, nh, pg), lambda i, j, pt: (layer, page_id(i, j, pt, p), 0, 0))

    per_seq = lambda a, b: pl.BlockSpec((1, a, b), lambda i, j, pt: (i, 0, 0))
    o = pl.pallas_call(
        functools.partial(_fox_sample_kernel, n_pg=n_pg, t_new=t, nh=nh, scale=hd ** -0.5),
        grid_spec=pltpu.PrefetchScalarGridSpec(
            num_scalar_prefetch=1,
            grid=(n, n_pages // n_pg),
            in_specs=[per_seq(nk, 2 * ncols), per_seq(pg, nk), per_seq(pg, nk), per_seq(pg, nh)]
            + [kv_page(p) for p in range(n_pg)] + [kv_page(p) for p in range(n_pg)]
            + [lf_page(p) for p in range(n_pg)],
            out_specs=per_seq(ncols, nk),
            scratch_shapes=[pltpu.VMEM((1, LANES), F32), pltpu.VMEM((1, LANES), F32), pltpu.VMEM((ncols, nk), F32),
                            pltpu.VMEM((1, LANES), F32), pltpu.VMEM((1, LANES), F32)]),
        out_shape=jax.ShapeDtypeStruct((n, ncols, nk), F32),
        compiler_params=_params("parallel", "arbitrary"),
        name="fox_sample",
    )(page_table, qb, padrows(k_new), padrows(v_new), padrows(lf_new), *([rows2d(cache_k)] * n_pg),
      *([rows2d(cache_v)] * n_pg),
      *([jnp.swapaxes(cache_lf, 2, 3)] * n_pg))
    o = o.reshape(n, t, n_kv, grp, n_kv, hd)
    return jnp.stack([o[:, :, kv, :, kv, :] for kv in range(n_kv)], axis=2).reshape(n, t, nq)


def _rope_tables(pos, hd):
    half = hd // 2
    inv = ROPE_THETA ** (-2.0 * jnp.arange(half, dtype=F32) / hd)
    ang = pos.astype(F32)[:, None] * inv[None, :]
    cos = jnp.concatenate([jnp.cos(ang), jnp.cos(ang)], axis=-1)
    sin = jnp.concatenate([-jnp.sin(ang), jnp.sin(ang)], axis=-1)
    reps = LANES // hd
    return jnp.tile(cos, (1, reps)), jnp.tile(sin, (1, reps))


def kernel(x_prompt, x_sample, p_prompt, p_sample, cache_conv, cache_swa_k, cache_swa_v, cache_fox_k, cache_fox_v, cache_fox_logf, page_table, norm_mix, norm_ffn, norm_ple, norm_final, conv_w_in, conv_w_dw, conv_b_dw, conv_ln_g, conv_ln_b, conv_w_out, swa_w_qkv, swa_sinks, swa_w_o, fox_w_in, fox_b_f, fox_w_o, ffn_w_gate, ffn_w_up, ffn_w_down, ple_w_proj, ple_w_gate):
    nb, seq, d = x_prompt.shape
    ns, ts, _ = x_sample.shape
    depth = norm_mix.shape[0]
    mp, ms = nb * seq, ns * ts
    m = mp + ms
    tm = _pick_tile(m, ROW_TILE_CAP, BF16_SUBLANES)
    kw = conv_w_dw.shape[1]
    window, swa_kv, swa_hd = cache_swa_k.shape[2:]
    swa_nq, swa_nk = swa_sinks.shape[1] * swa_hd, swa_kv * swa_hd
    fox_kv, fox_hd = cache_fox_k.shape[3:]
    fox_h = fox_b_f.shape[1]
    fox_nq, fox_nk = fox_h * fox_hd, fox_kv * fox_hd
    past_len = page_table.shape[1] * cache_fox_k.shape[2]

    x = jnp.concatenate([x_prompt.reshape(mp, d), x_sample.reshape(ms, d)], axis=0)
    pos = jnp.concatenate([jnp.tile(jnp.arange(seq, dtype=jnp.int32), nb),
                           jnp.tile(past_len + jnp.arange(ts, dtype=jnp.int32), ns)])
    cos, sin = _rope_tables(pos, swa_hd)
    vec = lambda a: a.reshape(1, -1)
    bf16 = lambda w: w.astype(BF16)
    conv_w_in, conv_w_out, swa_w_qkv, swa_w_o, fox_w_in, fox_w_o = map(
        bf16, (conv_w_in, conv_w_out, swa_w_qkv, swa_w_o, fox_w_in, fox_w_o))
    ffn_w_gate, ffn_w_up, ffn_w_down, ple_w_proj, ple_w_gate = map(
        bf16, (ffn_w_gate, ffn_w_up, ffn_w_down, ple_w_proj, ple_w_gate))
    p_p = p_prompt.reshape(depth, mp, -1)
    p_s = p_sample.reshape(depth, ms, -1)

    conv_p, conv_s = [], []
    swk_p, swv_p, swk_s, swv_s = [], [], [], []
    fk_p, fv_p, fl_p, fk_s, fv_s, fl_s = [], [], [], [], [], []
    for i in range(depth):
        kind, j = i % 3, i // 3
        g_mix = vec(norm_mix[i])
        if kind == 0:
            u = _conv_in(x, g_mix, conv_w_in, j, tm)
            dc = u.shape[1]
            conv_w = (conv_w_dw[j], vec(conv_b_dw[j]), vec(conv_ln_g[j]), vec(conv_ln_b[j]))
            o_p, c_p = _conv_core(u, jnp.zeros((nb, kw - 1, dc), F32), *conv_w, nb, seq, mp)
            o_s, c_s = _conv_core(u[mp:].reshape(ns, ts, dc), cache_conv[j], *conv_w, ns, ts, None)
            conv_p.append(c_p)
            conv_s.append(c_s)
            w_o = conv_w_out
        elif kind == 1:
            q, k, v = _swa_in(x, g_mix, swa_w_qkv, j, cos, sin, tm, swa_nq, swa_nk, swa_hd)
            o_p = _swa_prompt(q, k, v, swa_sinks[j], nb, seq, mp, swa_kv, swa_hd, window)
            o_s, ks_, vs_ = _swa_sample(q[mp:].reshape(ns, ts, swa_nq), k[mp:].reshape(ns, ts, swa_nk),
                                        v[mp:].reshape(ns, ts, swa_nk), cache_swa_k[j].reshape(ns, window, swa_nk),
                                        cache_swa_v[j].reshape(ns, window, swa_nk), swa_sinks[j], swa_kv, swa_hd,
                                        window, past_len)
            k_p, v_p = k[:mp].reshape(nb, seq, swa_kv, swa_hd), v[:mp].reshape(nb, seq, swa_kv, swa_hd)
            swk_p.append(k_p[:, seq - window:])
            swv_p.append(v_p[:, seq - window:])
            swk_s.append(ks_.reshape(ns, window, swa_kv, swa_hd))
            swv_s.append(vs_.reshape(ns, window, swa_kv, swa_hd))
            w_o = swa_w_o
        else:
            q, k, v, lf = _fox_in(x, g_mix, fox_w_in, j, vec(fox_b_f[j]), tm, fox_nq, fox_nk)
            c = _fox_cumsum(lf, nb, seq)
            ct = c.reshape(nb, seq, fox_kv, fox_h // fox_kv).transpose(0, 2, 3, 1)
            o_p = _fox_prompt(q, k, v, c, ct, nb, seq, mp, fox_kv, fox_hd)
            o_s = _fox_sample(q[mp:].reshape(ns, ts, fox_nq), k[mp:].reshape(ns, ts, fox_nk),
                              v[mp:].reshape(ns, ts, fox_nk), lf[mp:].reshape(ns, ts, fox_h),
                              cache_fox_k, cache_fox_v, cache_fox_logf, j, page_table, fox_kv, fox_hd)
            fk_p.append(k[:mp].reshape(nb, seq, fox_kv, fox_hd))
            fv_p.append(v[:mp].reshape(nb, seq, fox_kv, fox_hd))
            fl_p.append(lf[:mp].reshape(nb, seq, fox_h))
            fk_s.append(k[mp:].reshape(ns, ts, fox_kv, fox_hd))
            fv_s.append(v[mp:].reshape(ns, ts, fox_kv, fox_hd))
            fl_s.append(lf[mp:].reshape(ns, ts, fox_h))
            w_o = fox_w_o
        x = _out_proj(o_p, o_s.reshape(ms, -1).astype(BF16), w_o, j, x, tm)
        x = _ffn(x, vec(norm_ffn[i]), ffn_w_gate, ffn_w_up, ffn_w_down, i, tm)
        x = _ple(x, p_p, p_s, vec(norm_ple[i]), ple_w_proj, ple_w_gate, i, vec(norm_final), tm,
                 final=(i == depth - 1))

    return (x[:mp].reshape(nb, seq, d), x[mp:].reshape(ns, ts, d),
            jnp.stack(conv_p), jnp.stack(conv_s),
            jnp.stack(swk_p), jnp.stack(swv_p), jnp.stack(swk_s), jnp.stack(swv_s),
            jnp.stack(fk_p), jnp.stack(fv_p), jnp.stack(fl_p),
            jnp.stack(fk_s), jnp.stack(fv_s), jnp.stack(fl_s))
```

```python
import functools

import jax
import jax.numpy as jnp
from jax import lax
from jax.experimental import pallas as pl
from jax.experimental.pallas import tpu as pltpu

EPS = 1e-6
NEG = -1e30
ROPE_THETA = 10000.0
F32 = jnp.float32
BF16 = jnp.bfloat16

LANES = 128
SUBLANES = 8
BF16_SUBLANES = 16
VMEM_LIMIT = 56 * 1024 * 1024
ROW_TILE_CAP = 640
FFN_ROW_TILE_CAP = 1040
FFN_COL_TILE_CAP = 256
COL_CHUNK = 512
ATTN_BLOCK = 128
FOX_BLOCK_CAP = 256
FOX_SCORES_AHEAD = 4
CONV_TIME_TILE_CAP = 128
CONV_ROW_CHUNK = 32
FOX_PAGES_PER_STEP = 16


def _pick_tile(n, cap, mult):
    best = None
    for d in range(mult, min(n, cap) + 1, mult):
        if n % d == 0:
            best = d
    return best if best is not None else n


def _params(*sem):
    return pltpu.CompilerParams(dimension_semantics=sem, vmem_limit_bytes=VMEM_LIMIT)


def _resident(shape):
    nd = len(shape)
    return pl.BlockSpec(shape, lambda *_: (0,) * nd, pipeline_mode=pl.Buffered(1))


def _layer_resident(w, layer):
    return pl.BlockSpec((None,) + w.shape[1:], lambda *_: (layer, 0, 0), pipeline_mode=pl.Buffered(1))


def _rmsnorm(x, g):
    return x * lax.rsqrt(jnp.mean(x * x, axis=-1, keepdims=True) + EPS) * g


def _dot(a, b):
    return jnp.dot(a, b, preferred_element_type=F32)


def _dot_nt(a, b):
    return lax.dot_general(a, b, (((1,), (1,)), ((), ())), preferred_element_type=F32)


def _col_chunks(n, chunk=COL_CHUNK):
    return [(s, min(chunk, n - s)) for s in range(0, n, chunk)]


def _split3(x):
    hi = x.astype(BF16)
    r1 = x - hi.astype(F32)
    mid = r1.astype(BF16)
    lo = (r1 - mid.astype(F32)).astype(BF16)
    return hi, mid, lo


def _conv_in_kernel(x_ref, g_ref, w_ref, u_ref):
    h = _rmsnorm(x_ref[...], g_ref[...]).astype(BF16)
    dc = u_ref.shape[-1]
    for s, n in _col_chunks(dc):
        a = _dot(h, w_ref[:, s:s + n])
        b = _dot(h, w_ref[:, dc + s:dc + s + n])
        u_ref[:, s:s + n] = a * jax.nn.sigmoid(b)


def _conv_in(x, g, w, layer, tm):
    m, d = x.shape
    dc = w.shape[2] // 2
    return pl.pallas_call(
        _conv_in_kernel,
        grid=(m // tm,),
        in_specs=[pl.BlockSpec((tm, d), lambda i: (i, 0)), _resident((1, d)), _layer_resident(w, layer)],
        out_specs=pl.BlockSpec((tm, dc), lambda i: (i, 0)),
        out_shape=jax.ShapeDtypeStruct((m, dc), F32),
        compiler_params=_params("parallel"),
        name="conv_in",
    )(x, g, w)


def _swa_in_kernel(x_ref, g_ref, w_ref, cos_ref, sin_ref, q_ref, k_ref, v_ref, *, hd):
    h = _rmsnorm(x_ref[...], g_ref[...]).astype(BF16)
    cos = cos_ref[...]
    sin = sin_ref[...]
    lane = lax.broadcasted_iota(jnp.int32, cos.shape, 1)
    first_half = (lane % hd) < hd // 2

    def rope(z):
        partner = jnp.where(first_half, pltpu.roll(z, LANES - hd // 2, 1), pltpu.roll(z, hd // 2, 1))
        return z * cos + partner * sin

    nq, nk = q_ref.shape[-1], k_ref.shape[-1]
    for s, n in _col_chunks(nq):
        z = _dot(h, w_ref[:, s:s + n])
        for c in range(0, n, LANES):
            q_ref[:, s + c:s + c + LANES] = rope(z[:, c:c + LANES]).astype(q_ref.dtype)
    for s, n in _col_chunks(nk):
        z = _dot(h, w_ref[:, nq + s:nq + s + n])
        for c in range(0, n, LANES):
            k_ref[:, s + c:s + c + LANES] = rope(z[:, c:c + LANES])
    v_ref[...] = _dot(h, w_ref[:, nq + nk:])


def _swa_in(x, g, w, layer, cos, sin, tm, nq, nk, hd):
    m, d = x.shape
    assert LANES % hd == 0 and nq % LANES == 0 and nk % LANES == 0
    row = lambda n: pl.BlockSpec((tm, n), lambda i: (i, 0))
    return pl.pallas_call(
        functools.partial(_swa_in_kernel, hd=hd),
        grid=(m // tm,),
        in_specs=[row(d), _resident((1, d)), _layer_resident(w, layer), row(LANES), row(LANES)],
        out_specs=[row(nq), row(nk), row(nk)],
        out_shape=[jax.ShapeDtypeStruct((m, nq), BF16), jax.ShapeDtypeStruct((m, nk), F32),
                   jax.ShapeDtypeStruct((m, nk), F32)],
        compiler_params=_params("parallel"),
        name="swa_in",
    )(x, g, w, cos, sin)


def _fox_in_kernel(x_ref, g_ref, w_ref, bf_ref, q_ref, k_ref, v_ref, lf_ref):
    h = _rmsnorm(x_ref[...], g_ref[...]).astype(BF16)
    nq, nk = q_ref.shape[-1], k_ref.shape[-1]
    for s, n in _col_chunks(nq):
        q_ref[:, s:s + n] = _dot(h, w_ref[:, s:s + n]).astype(q_ref.dtype)
    for s, n in _col_chunks(nk):
        k_ref[:, s:s + n] = _dot(h, w_ref[:, nq + s:nq + s + n])
        v_ref[:, s:s + n] = _dot(h, w_ref[:, nq + nk + s:nq + nk + s + n])
    lf_ref[...] = jax.nn.log_sigmoid(_dot(h, w_ref[:, nq + 2 * nk:]) + bf_ref[...])


def _fox_in(x, g, w, layer, bf, tm, nq, nk):
    m, d = x.shape
    nh = w.shape[2] - nq - 2 * nk
    row = lambda n: pl.BlockSpec((tm, n), lambda i: (i, 0))
    return pl.pallas_call(
        _fox_in_kernel,
        grid=(m // tm,),
        in_specs=[row(d), _resident((1, d)), _layer_resident(w, layer), _resident((1, nh))],
        out_specs=[row(nq), row(nk), row(nk), row(nh)],
        out_shape=[jax.ShapeDtypeStruct((m, nq), BF16), jax.ShapeDtypeStruct((m, nk), F32),
                   jax.ShapeDtypeStruct((m, nk), F32), jax.ShapeDtypeStruct((m, nh), F32)],
        compiler_params=_params("parallel"),
        name="fox_in",
    )(x, g, w, bf)


def _out_proj_kernel(ap_ref, as_ref, w_ref, x_ref, o_ref):
    tm = x_ref.shape[0]
    ms = as_ref.shape[0]
    last = pl.num_programs(0) - 1

    def project(a):
        for s, n in _col_chunks(o_ref.shape[-1]):
            o_ref[:, s:s + n] = x_ref[:, s:s + n] + _dot(a, w_ref[:, s:s + n])

    @pl.when(pl.program_id(0) < last)
    def _():
        project(ap_ref[...])

    @pl.when(pl.program_id(0) == last)
    def _():
        project(jnp.concatenate([ap_ref[:tm - ms, :], as_ref[...]], axis=0))


def _out_proj(a_p, a_s, w, layer, x, tm):
    m, d = x.shape
    k = a_p.shape[1]
    ms = a_s.shape[0]
    assert a_p.shape[0] + ms == m and ms <= tm and (tm - ms) % BF16_SUBLANES == 0
    return pl.pallas_call(
        _out_proj_kernel,
        grid=(m // tm,),
        in_specs=[pl.BlockSpec((tm, k), lambda i: (i, 0)), _resident((ms, k)), _layer_resident(w, layer),
                  pl.BlockSpec((tm, d), lambda i: (i, 0))],
        out_specs=pl.BlockSpec((tm, d), lambda i: (i, 0)),
        out_shape=jax.ShapeDtypeStruct((m, d), F32),
        compiler_params=_params("parallel"),
        name="out_proj",
    )(a_p, a_s, w, x)


def _ffn_kernel(x_ref, g_ref, wg_ref, wu_ref, wd_ref, o_ref, h_ref):
    @pl.when(pl.program_id(1) == 0)
    def _():
        x = x_ref[...]
        h_ref[...] = _rmsnorm(x, g_ref[...]).astype(BF16)
        o_ref[...] = x

    h = h_ref[...]
    cast = lambda w_ref: w_ref[...].astype(BF16)
    a = (jax.nn.silu(_dot(h, cast(wg_ref))) * _dot(h, cast(wu_ref))).astype(BF16)
    o_ref[...] += _dot(a, cast(wd_ref))


def _ffn(x, g, wg, wu, wd, layer):
    m, d = x.shape
    f = wg.shape[2]
    tm = _pick_tile(m, FFN_ROW_TILE_CAP, BF16_SUBLANES)
    tf = _pick_tile(f, FFN_COL_TILE_CAP, LANES)
    return pl.pallas_call(
        _ffn_kernel,
        grid=(m // tm, f // tf),
        in_specs=[pl.BlockSpec((tm, d), lambda i, j: (i, 0)),
                  pl.BlockSpec((1, d), lambda i, j: (0, 0)),
                  pl.BlockSpec((None, d, tf), lambda i, j: (layer, 0, j)),
                  pl.BlockSpec((None, d, tf), lambda i, j: (layer, 0, j)),
                  pl.BlockSpec((None, tf, d), lambda i, j: (layer, j, 0))],
        out_specs=pl.BlockSpec((tm, d), lambda i, j: (i, 0)),
        out_shape=jax.ShapeDtypeStruct((m, d), F32),
        scratch_shapes=[pltpu.VMEM((tm, d), BF16)],
        compiler_params=_params("parallel", "arbitrary"),
        name="ffn",
    )(x, g, wg, wu, wd)


def _ple_kernel(x_ref, pp_ref, ps_ref, g_ref, wp_ref, wg_ref, gf_ref, o_ref, p_ref, *, final):
    tm = x_ref.shape[0]
    ms = ps_ref.shape[0]
    last = pl.num_programs(0) - 1

    @pl.when(pl.program_id(0) < last)
    def _():
        p_ref[...] = pp_ref[...].astype(BF16)

    @pl.when(pl.program_id(0) == last)
    def _():
        p_ref[:tm - ms, :] = pp_ref[:tm - ms, :].astype(BF16)
        p_ref[tm - ms:, :] = ps_ref[...].astype(BF16)

    h = _rmsnorm(x_ref[...], g_ref[...]).astype(BF16)
    p = p_ref[...]
    for s, n in _col_chunks(o_ref.shape[-1]):
        gate = jax.nn.sigmoid(_dot(h, wg_ref[:, s:s + n]))
        o_ref[:, s:s + n] = x_ref[:, s:s + n] + _dot(p, wp_ref[:, s:s + n]) * gate
    if final:
        o_ref[...] = _rmsnorm(o_ref[...], gf_ref[...])


def _ple(x, p_p, p_s, g, wp, wg, layer, gf, tm, final):
    m, d = x.shape
    dp = p_p.shape[2]
    ms = p_s.shape[1]
    assert p_p.shape[1] + ms == m and ms <= tm and (tm - ms) % BF16_SUBLANES == 0
    return pl.pallas_call(
        functools.partial(_ple_kernel, final=final),
        grid=(m // tm,),
        in_specs=[pl.BlockSpec((tm, d), lambda i: (i, 0)), pl.BlockSpec((None, tm, dp), lambda i: (layer, i, 0)),
                  _layer_resident(p_s, layer), _resident((1, d)), _layer_resident(wp, layer),
                  _layer_resident(wg, layer), _resident((1, d))],
        out_specs=pl.BlockSpec((tm, d), lambda i: (i, 0)),
        out_shape=jax.ShapeDtypeStruct((m, d), F32),
        scratch_shapes=[pltpu.VMEM((tm, dp), BF16)],
        compiler_params=_params("parallel"),
        name="ple_final" if final else "ple",
    )(x, p_p, p_s, g, wp, wg, gf)


def _conv_kernel(u_ref, hist_ref, w_ref, b_ref, lg_ref, lb_ref, y_ref, cache_ref, ext_ref, sh_ref, wb_ref, conv_ref,
                 *, tt, kw, pad):
    t = pl.program_id(1)
    off = pad - (kw - 1)

    sub = wb_ref.shape[1]

    @pl.when(t == 0)
    def _():
        ext_ref[off:pad, :] = hist_ref[0]
        for k in range(kw):
            wb_ref[k] = jnp.broadcast_to(w_ref[k:k + 1, :], wb_ref.shape[1:])

    @pl.when(t > 0)
    def _():
        ext_ref[off:pad, :] = ext_ref[tt + off:tt + pad, :]

    ext_ref[pad:pad + tt, :] = u_ref[...].reshape(tt, u_ref.shape[-1])
    cache_ref[0] = ext_ref[tt + off:tt + pad, :]

    d = ext_ref.shape[-1]
    rc = min(CONV_ROW_CHUNK, tt)
    rows, cn = sh_ref.shape[1:]
    for cs in range(0, d, cn):
        for b in range(1, SUBLANES):
            sh_ref[b - 1] = ext_ref[b:b + rows, cs:cs + cn]
        for r in range(0, tt, rc):
            accs = [jnp.broadcast_to(b_ref[:, cs:cs + cn], (sub, cn))] * (rc // sub)
            for k in range(kw):
                a, b = divmod(off + k, SUBLANES)
                wk = wb_ref[k, :, cs:cs + cn]
                for q in range(rc // sub):
                    r0 = a * SUBLANES + r + q * sub
                    src = ext_ref[r0:r0 + sub, cs:cs + cn] if b == 0 else sh_ref[b - 1, r0:r0 + sub, :]
                    accs[q] = accs[q] + src * wk
            for q in range(rc // sub):
                conv_ref[r + q * sub:r + (q + 1) * sub, cs:cs + cn] = accs[q]
    y = conv_ref[...]
    mu = jnp.mean(y, axis=-1, keepdims=True)
    yc = y - mu
    var = jnp.mean(yc * yc, axis=-1, keepdims=True)
    z = yc * lax.rsqrt(var + EPS) * lg_ref[...] + lb_ref[...]
    y_ref[...] = jax.nn.silu(z).astype(y_ref.dtype).reshape(y_ref.shape)


def _conv_core(u, hist, w_dw, b_dw, ln_g, ln_b, n, t, rows_out):
    d = u.shape[-1]
    kw = w_dw.shape[0]
    pad = -(-(kw - 1) // SUBLANES) * SUBLANES
    if u.ndim == 2:
        tt = _pick_tile(t, CONV_TIME_TILE_CAP, SUBLANES)
        nt = t // tt
        row_spec = pl.BlockSpec((tt, d), lambda i, j: (i * nt + j, 0))
        y_shape = (rows_out, d)
    else:
        tt = t
        row_spec = pl.BlockSpec((1, t, d), lambda i, j: (i, 0, 0))
        y_shape = (n, t, d)
    assert tt == t or tt >= kw - 1
    assert d % COL_CHUNK == 0
    vec = lambda: pl.BlockSpec((1, d), lambda i, j: (0, 0))
    return pl.pallas_call(
        functools.partial(_conv_kernel, tt=tt, kw=kw, pad=pad),
        grid=(n, t // tt),
        in_specs=[row_spec, pl.BlockSpec((1, kw - 1, d), lambda i, j: (i, 0, 0)),
                  pl.BlockSpec((kw, d), lambda i, j: (0, 0)), vec(), vec(), vec()],
        out_specs=[row_spec, pl.BlockSpec((1, kw - 1, d), lambda i, j: (i, 0, 0))],
        out_shape=[jax.ShapeDtypeStruct(y_shape, BF16), jax.ShapeDtypeStruct((n, kw - 1, d), F32)],
        scratch_shapes=[pltpu.VMEM((pad + tt, d), F32),
                        pltpu.VMEM((SUBLANES - 1, tt + pad - SUBLANES, COL_CHUNK), F32),
                        pltpu.VMEM((kw, min(SUBLANES, tt), d), F32), pltpu.VMEM((tt, d), F32)],
        compiler_params=_params("parallel", "arbitrary"),
        name="conv_core",
    )(u, hist, w_dw, b_dw, ln_g, ln_b)


def _swa_prompt_kernel(sink_ref, q_ref, kp_ref, kc_ref, vp_ref, vc_ref, o_ref, *, n_heads, n_kv, hd, window):
    i = pl.program_id(1)
    tq = q_ref.shape[0]
    grp = n_heads // n_kv
    key = lax.broadcasted_iota(jnp.int32, (2 * tq, grp * tq), 0)
    qry = lax.broadcasted_iota(jnp.int32, (2 * tq, grp * tq), 1) % tq
    dist = qry + tq - key
    valid = (dist >= 0) & (dist <= window) & ((key >= tq) | (i > 0))
    scale = hd ** -0.5
    kcat = jnp.concatenate([kp_ref[...], kc_ref[...]], axis=0).astype(BF16)
    vt = jnp.concatenate([vp_ref[...], vc_ref[...]], axis=0).T.astype(BF16)
    per_tile = LANES // hd
    for kv in range(n_kv):
        heads = range(kv * grp, (kv + 1) * grp)
        qs = jnp.concatenate([q_ref[:, h * hd:(h + 1) * hd] for h in heads], axis=0)
        sink = jnp.concatenate([jnp.full((1, tq), sink_ref[h], F32) for h in heads], axis=1)
        s = _dot_nt(kcat[:, kv * hd:(kv + 1) * hd], qs) * scale
        s = jnp.where(valid, s, NEG)
        m = jnp.maximum(jnp.max(s, axis=0, keepdims=True), sink)
        e = jnp.exp(s - m)
        den = jnp.sum(e, axis=0, keepdims=True) + jnp.exp(sink - m)
        ot = _dot(vt[kv * hd:(kv + 1) * hd, :], e.astype(BF16)) / den
        for g in range(0, grp, per_tile):
            tile = jnp.concatenate([ot[:, (g + t) * tq:(g + t + 1) * tq] for t in range(per_tile)], axis=0)
            c0 = (kv * grp + g) * hd
            o_ref[:, c0:c0 + LANES] = tile.T.astype(o_ref.dtype)


def _swa_prompt(q, k, v, sinks, n_seq, seq, rows_out, n_kv, hd, window):
    tq = ATTN_BLOCK
    assert seq % tq == 0 and window <= tq
    nb = seq // tq
    nq, nk = q.shape[1], k.shape[1]
    cur = lambda n: pl.BlockSpec((tq, n), lambda b, i: (b * nb + i, 0))
    prev = lambda n: pl.BlockSpec((tq, n), lambda b, i: (jnp.maximum(b * nb + i - 1, 0), 0))
    return pl.pallas_call(
        functools.partial(_swa_prompt_kernel, n_heads=nq // hd, n_kv=n_kv, hd=hd, window=window),
        grid=(n_seq, nb),
        in_specs=[pl.BlockSpec(memory_space=pltpu.SMEM), cur(nq), prev(nk), cur(nk), prev(nk), cur(nk)],
        out_specs=cur(nq),
        out_shape=jax.ShapeDtypeStruct((rows_out, nq), BF16),
        compiler_params=_params("parallel", "arbitrary"),
        name="swa_prompt",
    )(sinks, q, k, k, v, v)


def _swa_sample_kernel(q_ref, sink_ref, kc_ref, kn_ref, vc_ref, vn_ref, o_ref, ko_ref, vo_ref, *, hd, window, t_new,
                       past_len):
    nseq, n_kv, rows, _ = q_ref.shape
    nbuf = kc_ref.shape[1]
    ts = nbuf + t_new
    r = lax.broadcasted_iota(jnp.int32, (rows, ts), 0)
    c = lax.broadcasted_iota(jnp.int32, (rows, ts), 1)
    dist = (r % t_new) + nbuf - c
    valid = (dist >= 0) & (dist <= window) & (c >= nbuf - past_len)
    scale = hd ** -0.5
    for n in range(nseq):
        kall = jnp.concatenate([kc_ref[n], kn_ref[n]], axis=0)
        vall = jnp.concatenate([vc_ref[n], vn_ref[n]], axis=0)
        ko_ref[n] = kall[ts - nbuf:]
        vo_ref[n] = vall[ts - nbuf:]
        kb = kall.astype(BF16)
        vb = vall.astype(BF16)
        for kv in range(n_kv):
            s = _dot_nt(q_ref[n, kv], kb[:, kv * hd:(kv + 1) * hd]) * scale
            s = jnp.where(valid, s, NEG)
            sink = sink_ref[kv]
            m = jnp.maximum(jnp.max(s, axis=-1, keepdims=True), sink)
            e = jnp.exp(s - m)
            den = jnp.sum(e, axis=-1, keepdims=True) + jnp.exp(sink - m)
            p = (e / den).astype(BF16)
            o_ref[n, kv] = _dot(p, vb[:, kv * hd:(kv + 1) * hd]).astype(o_ref.dtype)


def _swa_sample(q, k_new, v_new, k_cache, v_cache, sinks, n_kv, hd, window, past_len):
    n, t, nq = q.shape
    grp = nq // hd // n_kv
    nbuf = k_cache.shape[1]
    nk = n_kv * hd
    rows = grp * t
    sb = _pick_tile(n, 8, 1)
    qr = q.reshape(n, t, n_kv, grp, hd).transpose(0, 2, 3, 1, 4).reshape(n, n_kv, rows, hd)
    sink_rows = jnp.broadcast_to(sinks.reshape(n_kv, grp, 1, 1), (n_kv, grp, t, 1)).reshape(n_kv, rows, 1)
    seq3 = lambda a, b: pl.BlockSpec((sb, a, b), lambda i: (i, 0, 0))
    seq4 = pl.BlockSpec((sb, n_kv, rows, hd), lambda i: (i, 0, 0, 0))
    o, ko, vo = pl.pallas_call(
        functools.partial(_swa_sample_kernel, hd=hd, window=window, t_new=t, past_len=past_len),
        grid=(n // sb,),
        in_specs=[seq4, _resident((n_kv, rows, 1)), seq3(nbuf, nk), seq3(t, nk), seq3(nbuf, nk), seq3(t, nk)],
        out_specs=[seq4, seq3(nbuf, nk), seq3(nbuf, nk)],
        out_shape=[jax.ShapeDtypeStruct((n, n_kv, rows, hd), BF16), jax.ShapeDtypeStruct((n, nbuf, nk), F32),
                   jax.ShapeDtypeStruct((n, nbuf, nk), F32)],
        compiler_params=_params("parallel"),
        name="swa_sample",
    )(qr, sink_rows, k_cache, k_new, v_cache, v_new)
    o = o.reshape(n, n_kv, grp, t, hd).transpose(0, 3, 1, 2, 4).reshape(n, t, nq)
    return o, ko, vo


def _tri3(n, kind):
    r = lax.broadcasted_iota(jnp.int32, (n, 3 * n), 0)
    c = lax.broadcasted_iota(jnp.int32, (n, 3 * n), 1) % n
    keep = (c <= r) if kind == "lower_incl" else (c > r)
    return jnp.where(keep, 1.0, 0.0).astype(BF16)


def _tri_sums(tri3, x):
    return _dot(tri3, jnp.concatenate(_split3(x), axis=0))


def _fox_cumsum_kernel(lf_ref, c_ref, *, blk):
    s, nh = lf_ref.shape
    tri = _tri3(blk, "lower_incl")

    def body(j, carry):
        r0 = pl.multiple_of(j * blk, blk)
        cs = _tri_sums(tri, lf_ref[pl.ds(r0, blk), :]) + carry
        c_ref[pl.ds(r0, blk), :] = cs
        return cs[blk - 1:blk, :]

    lax.fori_loop(0, s // blk, body, jnp.zeros((1, nh), F32))


def _fox_cumsum(lf, n_seq, seq):
    nh = lf.shape[1]
    blk = ATTN_BLOCK
    assert seq % blk == 0
    return pl.pallas_call(
        functools.partial(_fox_cumsum_kernel, blk=blk),
        grid=(n_seq,),
        in_specs=[pl.BlockSpec((seq, nh), lambda b: (b, 0))],
        out_specs=pl.BlockSpec((seq, nh), lambda b: (b, 0)),
        out_shape=jax.ShapeDtypeStruct((n_seq * seq, nh), F32),
        compiler_params=_params("parallel"),
        name="fox_cumsum",
    )(lf)


def _fox_prompt_kernel(q_ref, k_ref, v_ref, c_ref, ct_ref, o_ref, kb_ref, vt_ref, ckb_ref, m_ref, l_ref, acc_ref, *,
                       grp, hd):
    kv = pl.program_id(1)
    i = pl.program_id(2)
    tq = q_ref.shape[0]
    seq, nh = c_ref.shape
    scale = hd ** -0.5

    @pl.when(i == 0)
    def _():
        kb_ref[...] = k_ref[...].astype(BF16)
        for b in range(0, seq, LANES):
            vt_ref[:, b:b + LANES] = v_ref[b:b + LANES, :].T.astype(BF16)
        parts = _split3(c_ref[...])
        head = lax.broadcasted_iota(jnp.int32, (nh, LANES), 0)
        for g in range(grp):
            sel = jnp.where(head == kv * grp + g, 1.0, 0.0).astype(BF16)
            ckb_ref[g] = _dot(parts[0], sel) + _dot(parts[1], sel) + _dot(parts[2], sel)

    q0 = pl.multiple_of(i * tq, tq)
    cq = jnp.concatenate([ct_ref[0, 0, g:g + 1, pl.ds(q0, tq)] for g in range(grp)], axis=1)
    key_row = lax.broadcasted_iota(jnp.int32, (tq, LANES), 0)
    query = lax.broadcasted_iota(jnp.int32, (tq, LANES), 1)

    m_ref[...] = jnp.full(m_ref.shape, NEG, F32)
    l_ref[...] = jnp.zeros(l_ref.shape, F32)
    acc_ref[...] = jnp.zeros(acc_ref.shape, F32)

    def block(j, masked):
        k0 = pl.multiple_of(j * tq, tq)
        kb = kb_ref[pl.ds(k0, tq), :]
        vtb = vt_ref[:, pl.ds(k0, tq)]
        def scores(c):
            g, r0 = divmod(c, tq)
            return _dot_nt(kb, q_ref[r0:r0 + LANES, g * hd:(g + 1) * hd])

        starts = list(range(0, grp * tq, LANES))
        ahead = [scores(c) for c in starts[:FOX_SCORES_AHEAD]]
        for n, c in enumerate(starts):
            g, r0 = divmod(c, tq)
            cols = slice(c, c + LANES)
            if n + FOX_SCORES_AHEAD < len(starts):
                ahead.append(scores(starts[n + FOX_SCORES_AHEAD]))
            s = ahead[n] * scale + (cq[:, cols] - ckb_ref[g, pl.ds(k0, tq), :])
            if masked:
                s = jnp.where(key_row <= query + r0, s, NEG)
            m_old = m_ref[:, cols]
            m_new = jnp.maximum(m_old, jnp.max(s, axis=0, keepdims=True))
            alpha = jnp.exp(m_old - m_new)
            p = jnp.exp(s - m_new)
            l_ref[:, cols] = alpha * l_ref[:, cols] + jnp.sum(p, axis=0, keepdims=True)
            acc_ref[:, cols] = alpha * acc_ref[:, cols] + _dot(vtb, p.astype(BF16))
            m_ref[:, cols] = m_new

    def body(j, carry):
        block(j, False)
        return carry

    lax.fori_loop(0, i, body, 0)
    block(i, True)
    ot = acc_ref[...] / l_ref[...]
    for g in range(grp):
        o_ref[:, g * hd:(g + 1) * hd] = ot[:, g * tq:(g + 1) * tq].T.astype(o_ref.dtype)


def _fox_prompt(q, k, v, c, ct, n_seq, seq, rows_out, n_kv, hd):
    tq = _pick_tile(seq, FOX_BLOCK_CAP, LANES)
    nb = seq // tq
    nq = q.shape[1]
    nh = c.shape[1]
    grp = nq // hd // n_kv
    assert hd == LANES and tq % LANES == 0 and seq % tq == 0
    return pl.pallas_call(
        functools.partial(_fox_prompt_kernel, grp=grp, hd=hd),
        grid=(n_seq, n_kv, nb),
        in_specs=[pl.BlockSpec((tq, grp * hd), lambda b, kv, i: (b * nb + i, kv)),
                  pl.BlockSpec((seq, hd), lambda b, kv, i: (b, kv)),
                  pl.BlockSpec((seq, hd), lambda b, kv, i: (b, kv)),
                  pl.BlockSpec((seq, nh), lambda b, kv, i: (b, 0)),
                  pl.BlockSpec((1, 1, grp, seq), lambda b, kv, i: (b, kv, 0, 0))],
        out_specs=pl.BlockSpec((tq, grp * hd), lambda b, kv, i: (b * nb + i, kv)),
        out_shape=jax.ShapeDtypeStruct((rows_out, nq), BF16),
        scratch_shapes=[pltpu.VMEM((seq, hd), BF16), pltpu.VMEM((hd, seq), BF16), pltpu.VMEM((grp, seq, LANES), F32),
                        pltpu.VMEM((1, grp * tq), F32), pltpu.VMEM((1, grp * tq), F32),
                        pltpu.VMEM((hd, grp * tq), F32)],
        compiler_params=_params("parallel", "parallel", "arbitrary"),
        name="fox_prompt",
    )(q, k, v, c, ct)


def _fox_sample_kernel(pt_ref, q_ref, kn_ref, vn_ref, lfn_ref, *refs, n_pg, t_new, nh, scale):
    k_refs, v_refs, lf_refs = refs[:n_pg], refs[n_pg:2 * n_pg], refs[2 * n_pg:3 * n_pg]
    o_ref, m_ref, l_ref, acc_ref, run_ref, cn_ref = refs[3 * n_pg:]
    j = pl.program_id(1)
    pg = kn_ref.shape[1]
    ncols = t_new * nh
    assert 2 * ncols == LANES and pg == LANES
    lane = lax.broadcasted_iota(jnp.int32, (pg, LANES), 1)
    row = lax.broadcasted_iota(jnp.int32, (pg, LANES), 0)
    low = lane < ncols
    lane1 = lax.broadcasted_iota(jnp.int32, (1, LANES), 1)
    diag = row == lane
    qb = q_ref[0]

    def widen(lf_a, lf_b):
        return jnp.concatenate([lf_a] * t_new + [lf_b] * t_new, axis=-1)

    def to_rows(x):
        return jnp.sum(jnp.where(diag, x, 0.0), axis=-1, keepdims=True)

    def accumulate(s_tiles, v_pairs):
        m_old = m_ref[...]
        mx = m_old
        for s in s_tiles:
            mx = jnp.maximum(mx, jnp.max(s, axis=0, keepdims=True))
        m_new = jnp.maximum(mx, pltpu.roll(mx, ncols, 1))
        alpha = jnp.exp(m_old - m_new)
        l_new = alpha * l_ref[...]
        acc = to_rows(alpha)[:ncols] * acc_ref[...]
        for s, (va, vb) in zip(s_tiles, v_pairs):
            p = jnp.exp(s - m_new)
            l_new = l_new + jnp.sum(p, axis=0, keepdims=True)
            pt = p.T.astype(BF16)
            acc = acc + _dot(pt[:ncols], va) + _dot(pt[ncols:], vb)
        m_ref[...] = m_new
        l_ref[...] = l_new
        acc_ref[...] = acc

    @pl.when(j == 0)
    def _():
        lfw = widen(lfn_ref[0], lfn_ref[0])
        cnk = _tri_sums(_tri3(pg, "lower_incl"), lfw)
        tok = (lane % ncols) // nh
        cn = jnp.sum(jnp.where(row == tok, cnk, 0.0), axis=0, keepdims=True)
        cn_ref[...] = cn
        run_ref[...] = jnp.zeros(run_ref.shape, F32)
        m_ref[...] = jnp.full(m_ref.shape, NEG, F32)
        l_ref[...] = jnp.zeros(l_ref.shape, F32)
        acc_ref[...] = jnp.zeros(acc_ref.shape, F32)
        s = _dot(kn_ref[0].astype(BF16), qb) * scale + (cn - cnk)
        s = jnp.where((row <= tok) & low, s, NEG)
        vn = vn_ref[0].astype(BF16)
        accumulate([s], [(vn, vn)])

    k3 = lax.broadcasted_iota(jnp.int32, (3 * pg, pg), 0) % pg
    later3 = jnp.where(k3 > lax.broadcasted_iota(jnp.int32, (3 * pg, pg), 1), 1.0, 0.0).astype(BF16)
    col3 = lax.broadcasted_iota(jnp.int32, (LANES, 6 * nh), 0)
    r3 = lax.broadcasted_iota(jnp.int32, (LANES, 6 * nh), 1) % (2 * nh)
    spread3 = jnp.where(r3 == (col3 // ncols) * nh + col3 % nh, 1.0, 0.0).astype(BF16)
    colt = lax.broadcasted_iota(jnp.int32, (2 * nh, LANES), 1)
    spread_t = lax.broadcasted_iota(jnp.int32, (2 * nh, LANES), 0) == (colt // ncols) * nh + colt % nh
    s_tiles, v_pairs = [], []
    run = run_ref[...]
    n_kv = qb.shape[0] // k_refs[0].shape[1]

    def page_rows(ref):
        return jnp.concatenate([ref[pl.ds(kv, pg, stride=n_kv), :] for kv in range(n_kv)], axis=-1).astype(BF16)

    lft = jnp.concatenate([ref[...] for ref in lf_refs], axis=0)
    after_all = _dot(jnp.concatenate(_split3(lft), axis=1), later3)
    page_tot = jnp.sum(lft, axis=1, keepdims=True)
    qk = [_dot(page_rows(ref), qb) for ref in k_refs]
    for a in range(0, n_pg, 2):
        sa, sb = qk[a], qk[a + 1]
        after = after_all[a * nh:(a + 2) * nh]
        after_cols = _dot(spread3, jnp.concatenate(_split3(after), axis=0)).T
        tot = jnp.sum(jnp.where(spread_t, page_tot[a * nh:(a + 2) * nh], 0.0), axis=0, keepdims=True)
        tot_sw = pltpu.roll(tot, ncols, 1)
        bias = after_cols + run + jnp.where(lane1 >= ncols, tot_sw, 0.0) + cn_ref[...]
        s_tiles.append(jnp.where(low, sa, sb) * scale + bias)
        v_pairs.append((page_rows(v_refs[a]), page_rows(v_refs[a + 1])))
        run = run + tot + tot_sw
    run_ref[...] = run
    accumulate(s_tiles, v_pairs)

    @pl.when(j == pl.num_programs(1) - 1)
    def _():
        l_tot = l_ref[...] + pltpu.roll(l_ref[...], ncols, 1)
        o_ref[0] = acc_ref[...] / to_rows(l_tot)[:ncols]


def _fox_sample(q, k_new, v_new, lf_new, cache_k, cache_v, cache_lf, layer, page_table, n_kv, hd):
    n, t, nq = q.shape
    nh = nq // hd
    grp = nh // n_kv
    nk = n_kv * hd
    pg = cache_k.shape[2]
    n_pages = page_table.shape[1]
    ncols = t * nh
    n_pg = _pick_tile(n_pages, FOX_PAGES_PER_STEP, 2)
    assert n_pages % n_pg == 0 and n_pg % 2 == 0
    qt = q.reshape(n, t, n_kv, grp, hd).transpose(0, 4, 1, 2, 3)
    own = jnp.eye(n_kv, dtype=bool)[None, :, None, None, :, None]
    qb = jnp.where(own, qt[:, None], jnp.zeros((), q.dtype)).reshape(n, nk, ncols)
    qb = jnp.concatenate([qb, qb], axis=-1)
    padrows = lambda a: jnp.pad(a, ((0, 0), (0, pg - t), (0, 0)))

    rows2d = lambda c: c.reshape(c.shape[0], c.shape[1], pg * n_kv, hd)

    def page_id(i, j, pt, p):
        return pt[i, n_pages - 1 - (j * n_pg + p)]

    def kv_page(p):
        return pl.BlockSpec((None, None, pg * n_kv, hd), lambda i, j, pt: (layer, page_id(i, j, pt, p), 0, 0))

    def lf_page(p):
        return pl.BlockSpec((None, None, nh, pg), lambda i, j, pt: (layer, page_id(i, j, pt, p), 0, 0))

    per_seq = lambda a, b: pl.BlockSpec((1, a, b), lambda i, j, pt: (i, 0, 0))
    o = pl.pallas_call(
        functools.partial(_fox_sample_kernel, n_pg=n_pg, t_new=t, nh=nh, scale=hd ** -0.5),
        grid_spec=pltpu.PrefetchScalarGridSpec(
            num_scalar_prefetch=1,
            grid=(n, n_pages // n_pg),
            in_specs=[per_seq(nk, 2 * ncols), per_seq(pg, nk), per_seq(pg, nk), per_seq(pg, nh)]
            + [kv_page(p) for p in range(n_pg)] + [kv_page(p) for p in range(n_pg)]
            + [lf_page(p) for p in range(n_pg)],
            out_specs=per_seq(ncols, nk),
            scratch_shapes=[pltpu.VMEM((1, LANES), F32), pltpu.VMEM((1, LANES), F32), pltpu.VMEM((ncols, nk), F32),
                            pltpu.VMEM((1, LANES), F32), pltpu.VMEM((1, LANES), F32)]),
        out_shape=jax.ShapeDtypeStruct((n, ncols, nk), F32),
        compiler_params=_params("parallel", "arbitrary"),
        name="fox_sample",
    )(page_table, qb, padrows(k_new), padrows(v_new), padrows(lf_new), *([rows2d(cache_k)] * n_pg),
      *([rows2d(cache_v)] * n_pg),
      *([jnp.swapaxes(cache_lf, 2, 3)] * n_pg))
    o = o.reshape(n, t, n_kv, grp, n_kv, hd)
    return jnp.stack([o[:, :, kv, :, kv, :] for kv in range(n_kv)], axis=2).reshape(n, t, nq)


def _rope_tables(pos, hd):
    half = hd // 2
    inv = ROPE_THETA ** (-2.0 * jnp.arange(half, dtype=F32) / hd)
    ang = pos.astype(F32)[:, None] * inv[None, :]
    cos = jnp.concatenate([jnp.cos(ang), jnp.cos(ang)], axis=-1)
    sin = jnp.concatenate([-jnp.sin(ang), jnp.sin(ang)], axis=-1)
    reps = LANES // hd
    return jnp.tile(cos, (1, reps)), jnp.tile(sin, (1, reps))


def kernel(x_prompt, x_sample, p_prompt, p_sample, cache_conv, cache_swa_k, cache_swa_v, cache_fox_k, cache_fox_v, cache_fox_logf, page_table, norm_mix, norm_ffn, norm_ple, norm_final, conv_w_in, conv_w_dw, conv_b_dw, conv_ln_g, conv_ln_b, conv_w_out, swa_w_qkv, swa_sinks, swa_w_o, fox_w_in, fox_b_f, fox_w_o, ffn_w_gate, ffn_w_up, ffn_w_down, ple_w_proj, ple_w_gate):
    nb, seq, d = x_prompt.shape
    ns, ts, _ = x_sample.shape
    depth = norm_mix.shape[0]
    mp, ms = nb * seq, ns * ts
    m = mp + ms
    tm = _pick_tile(m, ROW_TILE_CAP, BF16_SUBLANES)
    kw = conv_w_dw.shape[1]
    window, swa_kv, swa_hd = cache_swa_k.shape[2:]
    swa_nq, swa_nk = swa_sinks.shape[1] * swa_hd, swa_kv * swa_hd
    fox_kv, fox_hd = cache_fox_k.shape[3:]
    fox_h = fox_b_f.shape[1]
    fox_nq, fox_nk = fox_h * fox_hd, fox_kv * fox_hd
    past_len = page_table.shape[1] * cache_fox_k.shape[2]

    x = jnp.concatenate([x_prompt.reshape(mp, d), x_sample.reshape(ms, d)], axis=0)
    pos = jnp.concatenate([jnp.tile(jnp.arange(seq, dtype=jnp.int32), nb),
                           jnp.tile(past_len + jnp.arange(ts, dtype=jnp.int32), ns)])
    cos, sin = _rope_tables(pos, swa_hd)
    vec = lambda a: a.reshape(1, -1)
    bf16 = lambda w: w.astype(BF16)
    conv_w_in, conv_w_out, swa_w_qkv, swa_w_o, fox_w_in, fox_w_o = map(
        bf16, (conv_w_in, conv_w_out, swa_w_qkv, swa_w_o, fox_w_in, fox_w_o))
    ple_w_proj, ple_w_gate = bf16(ple_w_proj), bf16(ple_w_gate)
    p_p = p_prompt.reshape(depth, mp, -1)
    p_s = p_sample.reshape(depth, ms, -1)

    conv_p, conv_s = [], []
    swk_p, swv_p, swk_s, swv_s = [], [], [], []
    fk_p, fv_p, fl_p, fk_s, fv_s, fl_s = [], [], [], [], [], []
    for i in range(depth):
        kind, j = i % 3, i // 3
        g_mix = vec(norm_mix[i])
        if kind == 0:
            u = _conv_in(x, g_mix, conv_w_in, j, tm)
            dc = u.shape[1]
            conv_w = (conv_w_dw[j], vec(conv_b_dw[j]), vec(conv_ln_g[j]), vec(conv_ln_b[j]))
            o_p, c_p = _conv_core(u, jnp.zeros((nb, kw - 1, dc), F32), *conv_w, nb, seq, mp)
            o_s, c_s = _conv_core(u[mp:].reshape(ns, ts, dc), cache_conv[j], *conv_w, ns, ts, None)
            conv_p.append(c_p)
            conv_s.append(c_s)
            w_o = conv_w_out
        elif kind == 1:
            q, k, v = _swa_in(x, g_mix, swa_w_qkv, j, cos, sin, tm, swa_nq, swa_nk, swa_hd)
            o_p = _swa_prompt(q, k, v, swa_sinks[j], nb, seq, mp, swa_kv, swa_hd, window)
            o_s, ks_, vs_ = _swa_sample(q[mp:].reshape(ns, ts, swa_nq), k[mp:].reshape(ns, ts, swa_nk),
                                        v[mp:].reshape(ns, ts, swa_nk), cache_swa_k[j].reshape(ns, window, swa_nk),
                                        cache_swa_v[j].reshape(ns, window, swa_nk), swa_sinks[j], swa_kv, swa_hd,
                                        window, past_len)
            k_p, v_p = k[:mp].reshape(nb, seq, swa_kv, swa_hd), v[:mp].reshape(nb, seq, swa_kv, swa_hd)
            swk_p.append(k_p[:, seq - window:])
            swv_p.append(v_p[:, seq - window:])
            swk_s.append(ks_.reshape(ns, window, swa_kv, swa_hd))
            swv_s.append(vs_.reshape(ns, window, swa_kv, swa_hd))
            w_o = swa_w_o
        else:
            q, k, v, lf = _fox_in(x, g_mix, fox_w_in, j, vec(fox_b_f[j]), tm, fox_nq, fox_nk)
            c = _fox_cumsum(lf, nb, seq)
            ct = c.reshape(nb, seq, fox_kv, fox_h // fox_kv).transpose(0, 2, 3, 1)
            o_p = _fox_prompt(q, k, v, c, ct, nb, seq, mp, fox_kv, fox_hd)
            o_s = _fox_sample(q[mp:].reshape(ns, ts, fox_nq), k[mp:].reshape(ns, ts, fox_nk),
                              v[mp:].reshape(ns, ts, fox_nk), lf[mp:].reshape(ns, ts, fox_h),
                              cache_fox_k, cache_fox_v, cache_fox_logf, j, page_table, fox_kv, fox_hd)
            fk_p.append(k[:mp].reshape(nb, seq, fox_kv, fox_hd))
            fv_p.append(v[:mp].reshape(nb, seq, fox_kv, fox_hd))
            fl_p.append(lf[:mp].reshape(nb, seq, fox_h))
            fk_s.append(k[mp:].reshape(ns, ts, fox_kv, fox_hd))
            fv_s.append(v[mp:].reshape(ns, ts, fox_kv, fox_hd))
            fl_s.append(lf[mp:].reshape(ns, ts, fox_h))
            w_o = fox_w_o
        x = _out_proj(o_p, o_s.reshape(ms, -1).astype(BF16), w_o, j, x, tm)
        x = _ffn(x, vec(norm_ffn[i]), ffn_w_gate, ffn_w_up, ffn_w_down, i)
        x = _ple(x, p_p, p_s, vec(norm_ple[i]), ple_w_proj, ple_w_gate, i, vec(norm_final), tm,
                 final=(i == depth - 1))

    return (x[:mp].reshape(nb, seq, d), x[mp:].reshape(ns, ts, d),
            jnp.stack(conv_p), jnp.stack(conv_s),
            jnp.stack(swk_p), jnp.stack(swv_p), jnp.stack(swk_s), jnp.stack(swv_s),
            jnp.stack(fk_p), jnp.stack(fv_p), jnp.stack(fl_p),
            jnp.stack(fk_s), jnp.stack(fv_s), jnp.stack(fl_s))
```

```python
import functools

import jax
import jax.numpy as jnp
from jax import lax
from jax.experimental import pallas as pl
from jax.experimental.pallas import tpu as pltpu

EPS = 1e-6
NEG = -1e30
ROPE_THETA = 10000.0
F32 = jnp.float32
BF16 = jnp.bfloat16

LANES = 128
SUBLANES = 8
BF16_SUBLANES = 16
VMEM_LIMIT = 60 * 1024 * 1024
ROW_TILE_CAP = 640
FFN_ROW_TILE_CAP = 1040
FFN_COL_TILE_CAP = 512
COL_CHUNK = 512
ATTN_BLOCK = 128
FOX_BLOCK_CAP = 256
FOX_SCORES_AHEAD = 4
CONV_TIME_TILE_CAP = 128
CONV_ROW_CHUNK = 32
FOX_PAGES_PER_STEP = 32


def _pick_tile(n, cap, mult):
    best = None
    for d in range(mult, min(n, cap) + 1, mult):
        if n % d == 0:
            best = d
    return best if best is not None else n


def _params(*sem):
    return pltpu.CompilerParams(dimension_semantics=sem, vmem_limit_bytes=VMEM_LIMIT)


def _resident(shape):
    nd = len(shape)
    return pl.BlockSpec(shape, lambda *_: (0,) * nd, pipeline_mode=pl.Buffered(1))


def _layer_resident(w, layer):
    return pl.BlockSpec((None,) + w.shape[1:], lambda *_: (layer, 0, 0), pipeline_mode=pl.Buffered(1))


def _rmsnorm(x, g):
    return x * lax.rsqrt(jnp.mean(x * x, axis=-1, keepdims=True) + EPS) * g


def _dot(a, b):
    return jnp.dot(a, b, preferred_element_type=F32)


def _dot_nt(a, b):
    return lax.dot_general(a, b, (((1,), (1,)), ((), ())), preferred_element_type=F32)


def _col_chunks(n, chunk=COL_CHUNK):
    return [(s, min(chunk, n - s)) for s in range(0, n, chunk)]


def _split3(x):
    hi = x.astype(BF16)
    r1 = x - hi.astype(F32)
    mid = r1.astype(BF16)
    lo = (r1 - mid.astype(F32)).astype(BF16)
    return hi, mid, lo


def _conv_in_kernel(x_ref, g_ref, w_ref, u_ref):
    h = _rmsnorm(x_ref[...], g_ref[...]).astype(BF16)
    dc = u_ref.shape[-1]
    for s, n in _col_chunks(dc):
        a = _dot(h, w_ref[:, s:s + n])
        b = _dot(h, w_ref[:, dc + s:dc + s + n])
        u_ref[:, s:s + n] = a * jax.nn.sigmoid(b)


def _conv_in(x, g, w, layer, tm):
    m, d = x.shape
    dc = w.shape[2] // 2
    return pl.pallas_call(
        _conv_in_kernel,
        grid=(m // tm,),
        in_specs=[pl.BlockSpec((tm, d), lambda i: (i, 0)), _resident((1, d)), _layer_resident(w, layer)],
        out_specs=pl.BlockSpec((tm, dc), lambda i: (i, 0)),
        out_shape=jax.ShapeDtypeStruct((m, dc), F32),
        compiler_params=_params("parallel"),
        name="conv_in",
    )(x, g, w)


def _swa_in_kernel(x_ref, g_ref, w_ref, cos_ref, sin_ref, q_ref, k_ref, v_ref, *, hd):
    h = _rmsnorm(x_ref[...], g_ref[...]).astype(BF16)
    cos = cos_ref[...]
    sin = sin_ref[...]
    lane = lax.broadcasted_iota(jnp.int32, cos.shape, 1)
    first_half = (lane % hd) < hd // 2

    def rope(z):
        partner = jnp.where(first_half, pltpu.roll(z, LANES - hd // 2, 1), pltpu.roll(z, hd // 2, 1))
        return z * cos + partner * sin

    nq, nk = q_ref.shape[-1], k_ref.shape[-1]
    for s, n in _col_chunks(nq):
        z = _dot(h, w_ref[:, s:s + n])
        for c in range(0, n, LANES):
            q_ref[:, s + c:s + c + LANES] = rope(z[:, c:c + LANES]).astype(q_ref.dtype)
    for s, n in _col_chunks(nk):
        z = _dot(h, w_ref[:, nq + s:nq + s + n])
        for c in range(0, n, LANES):
            k_ref[:, s + c:s + c + LANES] = rope(z[:, c:c + LANES])
    v_ref[...] = _dot(h, w_ref[:, nq + nk:])


def _swa_in(x, g, w, layer, cos, sin, tm, nq, nk, hd):
    m, d = x.shape
    assert LANES % hd == 0 and nq % LANES == 0 and nk % LANES == 0
    row = lambda n: pl.BlockSpec((tm, n), lambda i: (i, 0))
    return pl.pallas_call(
        functools.partial(_swa_in_kernel, hd=hd),
        grid=(m // tm,),
        in_specs=[row(d), _resident((1, d)), _layer_resident(w, layer), row(LANES), row(LANES)],
        out_specs=[row(nq), row(nk), row(nk)],
        out_shape=[jax.ShapeDtypeStruct((m, nq), BF16), jax.ShapeDtypeStruct((m, nk), F32),
                   jax.ShapeDtypeStruct((m, nk), F32)],
        compiler_params=_params("parallel"),
        name="swa_in",
    )(x, g, w, cos, sin)


def _fox_in_kernel(x_ref, g_ref, w_ref, bf_ref, q_ref, k_ref, v_ref, lf_ref):
    h = _rmsnorm(x_ref[...], g_ref[...]).astype(BF16)
    nq, nk = q_ref.shape[-1], k_ref.shape[-1]
    for s, n in _col_chunks(nq):
        q_ref[:, s:s + n] = _dot(h, w_ref[:, s:s + n]).astype(q_ref.dtype)
    for s, n in _col_chunks(nk):
        k_ref[:, s:s + n] = _dot(h, w_ref[:, nq + s:nq + s + n])
        v_ref[:, s:s + n] = _dot(h, w_ref[:, nq + nk + s:nq + nk + s + n])
    lf_ref[...] = jax.nn.log_sigmoid(_dot(h, w_ref[:, nq + 2 * nk:]) + bf_ref[...])


def _fox_in(x, g, w, layer, bf, tm, nq, nk):
    m, d = x.shape
    nh = w.shape[2] - nq - 2 * nk
    row = lambda n: pl.BlockSpec((tm, n), lambda i: (i, 0))
    return pl.pallas_call(
        _fox_in_kernel,
        grid=(m // tm,),
        in_specs=[row(d), _resident((1, d)), _layer_resident(w, layer), _resident((1, nh))],
        out_specs=[row(nq), row(nk), row(nk), row(nh)],
        out_shape=[jax.ShapeDtypeStruct((m, nq), BF16), jax.ShapeDtypeStruct((m, nk), F32),
                   jax.ShapeDtypeStruct((m, nk), F32), jax.ShapeDtypeStruct((m, nh), F32)],
        compiler_params=_params("parallel"),
        name="fox_in",
    )(x, g, w, bf)


def _out_proj_kernel(ap_ref, as_ref, w_ref, x_ref, o_ref):
    tm = x_ref.shape[0]
    ms = as_ref.shape[0]
    last = pl.num_programs(0) - 1

    def project(a):
        for s, n in _col_chunks(o_ref.shape[-1]):
            o_ref[:, s:s + n] = x_ref[:, s:s + n] + _dot(a, w_ref[:, s:s + n])

    @pl.when(pl.program_id(0) < last)
    def _():
        project(ap_ref[...])

    @pl.when(pl.program_id(0) == last)
    def _():
        project(jnp.concatenate([ap_ref[:tm - ms, :], as_ref[...]], axis=0))


def _out_proj(a_p, a_s, w, layer, x, tm):
    m, d = x.shape
    k = a_p.shape[1]
    ms = a_s.shape[0]
    assert a_p.shape[0] + ms == m and ms <= tm and (tm - ms) % BF16_SUBLANES == 0
    return pl.pallas_call(
        _out_proj_kernel,
        grid=(m // tm,),
        in_specs=[pl.BlockSpec((tm, k), lambda i: (i, 0)), _resident((ms, k)), _layer_resident(w, layer),
                  pl.BlockSpec((tm, d), lambda i: (i, 0))],
        out_specs=pl.BlockSpec((tm, d), lambda i: (i, 0)),
        out_shape=jax.ShapeDtypeStruct((m, d), F32),
        compiler_params=_params("parallel"),
        name="out_proj",
    )(a_p, a_s, w, x)


def _ffn_kernel(x_ref, g_ref, wg_ref, wu_ref, wd_ref, o_ref, h_ref):
    @pl.when(pl.program_id(1) == 0)
    def _():
        x = x_ref[...]
        h_ref[...] = _rmsnorm(x, g_ref[...]).astype(BF16)
        o_ref[...] = x

    h = h_ref[...]
    cast = lambda w_ref: w_ref[...].astype(BF16)
    a = (jax.nn.silu(_dot(h, cast(wg_ref))) * _dot(h, cast(wu_ref))).astype(BF16)
    o_ref[...] += _dot(a, cast(wd_ref))


def _ffn(x, g, wg, wu, wd, layer):
    m, d = x.shape
    f = wg.shape[2]
    tm = _pick_tile(m, FFN_ROW_TILE_CAP, BF16_SUBLANES)
    tf = _pick_tile(f, FFN_COL_TILE_CAP, LANES)
    return pl.pallas_call(
        _ffn_kernel,
        grid=(m // tm, f // tf),
        in_specs=[pl.BlockSpec((tm, d), lambda i, j: (i, 0), pipeline_mode=pl.Buffered(1)),
                  pl.BlockSpec((1, d), lambda i, j: (0, 0)),
                  pl.BlockSpec((None, d, tf), lambda i, j: (layer, 0, j)),
                  pl.BlockSpec((None, d, tf), lambda i, j: (layer, 0, j)),
                  pl.BlockSpec((None, tf, d), lambda i, j: (layer, j, 0))],
        out_specs=pl.BlockSpec((tm, d), lambda i, j: (i, 0)),
        out_shape=jax.ShapeDtypeStruct((m, d), F32),
        scratch_shapes=[pltpu.VMEM((tm, d), BF16)],
        compiler_params=_params("parallel", "arbitrary"),
        name="ffn",
    )(x, g, wg, wu, wd)


def _ple_kernel(x_ref, pp_ref, ps_ref, g_ref, wp_ref, wg_ref, gf_ref, o_ref, p_ref, *, final):
    tm = x_ref.shape[0]
    ms = ps_ref.shape[0]
    last = pl.num_programs(0) - 1

    @pl.when(pl.program_id(0) < last)
    def _():
        p_ref[...] = pp_ref[...].astype(BF16)

    @pl.when(pl.program_id(0) == last)
    def _():
        p_ref[:tm - ms, :] = pp_ref[:tm - ms, :].astype(BF16)
        p_ref[tm - ms:, :] = ps_ref[...].astype(BF16)

    h = _rmsnorm(x_ref[...], g_ref[...]).astype(BF16)
    p = p_ref[...]
    for s, n in _col_chunks(o_ref.shape[-1]):
        gate = jax.nn.sigmoid(_dot(h, wg_ref[:, s:s + n]))
        o_ref[:, s:s + n] = x_ref[:, s:s + n] + _dot(p, wp_ref[:, s:s + n]) * gate
    if final:
        o_ref[...] = _rmsnorm(o_ref[...], gf_ref[...])


def _ple(x, p_p, p_s, g, wp, wg, layer, gf, tm, final):
    m, d = x.shape
    dp = p_p.shape[2]
    ms = p_s.shape[1]
    assert p_p.shape[1] + ms == m and ms <= tm and (tm - ms) % BF16_SUBLANES == 0
    return pl.pallas_call(
        functools.partial(_ple_kernel, final=final),
        grid=(m // tm,),
        in_specs=[pl.BlockSpec((tm, d), lambda i: (i, 0)), pl.BlockSpec((None, tm, dp), lambda i: (layer, i, 0)),
                  _layer_resident(p_s, layer), _resident((1, d)), _layer_resident(wp, layer),
                  _layer_resident(wg, layer), _resident((1, d))],
        out_specs=pl.BlockSpec((tm, d), lambda i: (i, 0)),
        out_shape=jax.ShapeDtypeStruct((m, d), F32),
        scratch_shapes=[pltpu.VMEM((tm, dp), BF16)],
        compiler_params=_params("parallel"),
        name="ple_final" if final else "ple",
    )(x, p_p, p_s, g, wp, wg, gf)


def _conv_kernel(u_ref, hist_ref, w_ref, b_ref, lg_ref, lb_ref, y_ref, cache_ref, ext_ref, sh_ref, wb_ref, conv_ref,
                 *, tt, kw, pad):
    t = pl.program_id(1)
    off = pad - (kw - 1)

    sub = wb_ref.shape[1]

    @pl.when(t == 0)
    def _():
        ext_ref[off:pad, :] = hist_ref[0]
        for k in range(kw):
            wb_ref[k] = jnp.broadcast_to(w_ref[k:k + 1, :], wb_ref.shape[1:])

    @pl.when(t > 0)
    def _():
        ext_ref[off:pad, :] = ext_ref[tt + off:tt + pad, :]

    ext_ref[pad:pad + tt, :] = u_ref[...].reshape(tt, u_ref.shape[-1])
    cache_ref[0] = ext_ref[tt + off:tt + pad, :]

    d = ext_ref.shape[-1]
    rc = min(CONV_ROW_CHUNK, tt)
    rows, cn = sh_ref.shape[1:]
    for cs in range(0, d, cn):
        for b in range(1, SUBLANES):
            sh_ref[b - 1] = ext_ref[b:b + rows, cs:cs + cn]
        for r in range(0, tt, rc):
            accs = [jnp.broadcast_to(b_ref[:, cs:cs + cn], (sub, cn))] * (rc // sub)
            for k in range(kw):
                a, b = divmod(off + k, SUBLANES)
                wk = wb_ref[k, :, cs:cs + cn]
                for q in range(rc // sub):
                    r0 = a * SUBLANES + r + q * sub
                    src = ext_ref[r0:r0 + sub, cs:cs + cn] if b == 0 else sh_ref[b - 1, r0:r0 + sub, :]
                    accs[q] = accs[q] + src * wk
            for q in range(rc // sub):
                conv_ref[r + q * sub:r + (q + 1) * sub, cs:cs + cn] = accs[q]
    y = conv_ref[...]
    mu = jnp.mean(y, axis=-1, keepdims=True)
    yc = y - mu
    var = jnp.mean(yc * yc, axis=-1, keepdims=True)
    z = yc * lax.rsqrt(var + EPS) * lg_ref[...] + lb_ref[...]
    y_ref[...] = jax.nn.silu(z).astype(y_ref.dtype).reshape(y_ref.shape)


def _conv_core(u, hist, w_dw, b_dw, ln_g, ln_b, n, t, rows_out):
    d = u.shape[-1]
    kw = w_dw.shape[0]
    pad = -(-(kw - 1) // SUBLANES) * SUBLANES
    if u.ndim == 2:
        tt = _pick_tile(t, CONV_TIME_TILE_CAP, SUBLANES)
        nt = t // tt
        row_spec = pl.BlockSpec((tt, d), lambda i, j: (i * nt + j, 0))
        y_shape = (rows_out, d)
    else:
        tt = t
        row_spec = pl.BlockSpec((1, t, d), lambda i, j: (i, 0, 0))
        y_shape = (n, t, d)
    assert tt == t or tt >= kw - 1
    assert d % COL_CHUNK == 0
    vec = lambda: pl.BlockSpec((1, d), lambda i, j: (0, 0))
    return pl.pallas_call(
        functools.partial(_conv_kernel, tt=tt, kw=kw, pad=pad),
        grid=(n, t // tt),
        in_specs=[row_spec, pl.BlockSpec((1, kw - 1, d), lambda i, j: (i, 0, 0)),
                  pl.BlockSpec((kw, d), lambda i, j: (0, 0)), vec(), vec(), vec()],
        out_specs=[row_spec, pl.BlockSpec((1, kw - 1, d), lambda i, j: (i, 0, 0))],
        out_shape=[jax.ShapeDtypeStruct(y_shape, BF16), jax.ShapeDtypeStruct((n, kw - 1, d), F32)],
        scratch_shapes=[pltpu.VMEM((pad + tt, d), F32),
                        pltpu.VMEM((SUBLANES - 1, tt + pad - SUBLANES, COL_CHUNK), F32),
                        pltpu.VMEM((kw, min(SUBLANES, tt), d), F32), pltpu.VMEM((tt, d), F32)],
        compiler_params=_params("parallel", "arbitrary"),
        name="conv_core",
    )(u, hist, w_dw, b_dw, ln_g, ln_b)


def _swa_prompt_kernel(sink_ref, q_ref, kp_ref, kc_ref, vp_ref, vc_ref, o_ref, *, n_heads, n_kv, hd, window):
    i = pl.program_id(1)
    tq = q_ref.shape[0]
    grp = n_heads // n_kv
    key = lax.broadcasted_iota(jnp.int32, (2 * tq, grp * tq), 0)
    qry = lax.broadcasted_iota(jnp.int32, (2 * tq, grp * tq), 1) % tq
    dist = qry + tq - key
    valid = (dist >= 0) & (dist <= window) & ((key >= tq) | (i > 0))
    scale = hd ** -0.5
    kcat = jnp.concatenate([kp_ref[...], kc_ref[...]], axis=0).astype(BF16)
    vt = jnp.concatenate([vp_ref[...], vc_ref[...]], axis=0).T.astype(BF16)
    per_tile = LANES // hd
    for kv in range(n_kv):
        heads = range(kv * grp, (kv + 1) * grp)
        qs = jnp.concatenate([q_ref[:, h * hd:(h + 1) * hd] for h in heads], axis=0)
        sink = jnp.concatenate([jnp.full((1, tq), sink_ref[h], F32) for h in heads], axis=1)
        s = _dot_nt(kcat[:, kv * hd:(kv + 1) * hd], qs) * scale
        s = jnp.where(valid, s, NEG)
        m = jnp.maximum(jnp.max(s, axis=0, keepdims=True), sink)
        e = jnp.exp(s - m)
        den = jnp.sum(e, axis=0, keepdims=True) + jnp.exp(sink - m)
        ot = _dot(vt[kv * hd:(kv + 1) * hd, :], e.astype(BF16)) / den
        for g in range(0, grp, per_tile):
            tile = jnp.concatenate([ot[:, (g + t) * tq:(g + t + 1) * tq] for t in range(per_tile)], axis=0)
            c0 = (kv * grp + g) * hd
            o_ref[:, c0:c0 + LANES] = tile.T.astype(o_ref.dtype)


def _swa_prompt(q, k, v, sinks, n_seq, seq, rows_out, n_kv, hd, window):
    tq = ATTN_BLOCK
    assert seq % tq == 0 and window <= tq
    nb = seq // tq
    nq, nk = q.shape[1], k.shape[1]
    cur = lambda n: pl.BlockSpec((tq, n), lambda b, i: (b * nb + i, 0))
    prev = lambda n: pl.BlockSpec((tq, n), lambda b, i: (jnp.maximum(b * nb + i - 1, 0), 0))
    return pl.pallas_call(
        functools.partial(_swa_prompt_kernel, n_heads=nq // hd, n_kv=n_kv, hd=hd, window=window),
        grid=(n_seq, nb),
        in_specs=[pl.BlockSpec(memory_space=pltpu.SMEM), cur(nq), prev(nk), cur(nk), prev(nk), cur(nk)],
        out_specs=cur(nq),
        out_shape=jax.ShapeDtypeStruct((rows_out, nq), BF16),
        compiler_params=_params("parallel", "arbitrary"),
        name="swa_prompt",
    )(sinks, q, k, k, v, v)


def _swa_sample_kernel(q_ref, sink_ref, kc_ref, kn_ref, vc_ref, vn_ref, o_ref, ko_ref, vo_ref, *, hd, window, t_new,
                       past_len):
    nseq, n_kv, rows, _ = q_ref.shape
    nbuf = kc_ref.shape[1]
    ts = nbuf + t_new
    r = lax.broadcasted_iota(jnp.int32, (rows, ts), 0)
    c = lax.broadcasted_iota(jnp.int32, (rows, ts), 1)
    dist = (r % t_new) + nbuf - c
    valid = (dist >= 0) & (dist <= window) & (c >= nbuf - past_len)
    scale = hd ** -0.5
    for n in range(nseq):
        kall = jnp.concatenate([kc_ref[n], kn_ref[n]], axis=0)
        vall = jnp.concatenate([vc_ref[n], vn_ref[n]], axis=0)
        ko_ref[n] = kall[ts - nbuf:]
        vo_ref[n] = vall[ts - nbuf:]
        kb = kall.astype(BF16)
        vb = vall.astype(BF16)
        for kv in range(n_kv):
            s = _dot_nt(q_ref[n, kv], kb[:, kv * hd:(kv + 1) * hd]) * scale
            s = jnp.where(valid, s, NEG)
            sink = sink_ref[kv]
            m = jnp.maximum(jnp.max(s, axis=-1, keepdims=True), sink)
            e = jnp.exp(s - m)
            den = jnp.sum(e, axis=-1, keepdims=True) + jnp.exp(sink - m)
            p = (e / den).astype(BF16)
            o_ref[n, kv] = _dot(p, vb[:, kv * hd:(kv + 1) * hd]).astype(o_ref.dtype)


def _swa_sample(q, k_new, v_new, k_cache, v_cache, sinks, n_kv, hd, window, past_len):
    n, t, nq = q.shape
    grp = nq // hd // n_kv
    nbuf = k_cache.shape[1]
    nk = n_kv * hd
    rows = grp * t
    sb = _pick_tile(n, 8, 1)
    qr = q.reshape(n, t, n_kv, grp, hd).transpose(0, 2, 3, 1, 4).reshape(n, n_kv, rows, hd)
    sink_rows = jnp.broadcast_to(sinks.reshape(n_kv, grp, 1, 1), (n_kv, grp, t, 1)).reshape(n_kv, rows, 1)
    seq3 = lambda a, b: pl.BlockSpec((sb, a, b), lambda i: (i, 0, 0))
    seq4 = pl.BlockSpec((sb, n_kv, rows, hd), lambda i: (i, 0, 0, 0))
    o, ko, vo = pl.pallas_call(
        functools.partial(_swa_sample_kernel, hd=hd, window=window, t_new=t, past_len=past_len),
        grid=(n // sb,),
        in_specs=[seq4, _resident((n_kv, rows, 1)), seq3(nbuf, nk), seq3(t, nk), seq3(nbuf, nk), seq3(t, nk)],
        out_specs=[seq4, seq3(nbuf, nk), seq3(nbuf, nk)],
        out_shape=[jax.ShapeDtypeStruct((n, n_kv, rows, hd), BF16), jax.ShapeDtypeStruct((n, nbuf, nk), F32),
                   jax.ShapeDtypeStruct((n, nbuf, nk), F32)],
        compiler_params=_params("parallel"),
        name="swa_sample",
    )(qr, sink_rows, k_cache, k_new, v_cache, v_new)
    o = o.reshape(n, n_kv, grp, t, hd).transpose(0, 3, 1, 2, 4).reshape(n, t, nq)
    return o, ko, vo


def _tri3(n, kind):
    r = lax.broadcasted_iota(jnp.int32, (n, 3 * n), 0)
    c = lax.broadcasted_iota(jnp.int32, (n, 3 * n), 1) % n
    keep = (c <= r) if kind == "lower_incl" else (c > r)
    return jnp.where(keep, 1.0, 0.0).astype(BF16)


def _tri_sums(tri3, x):
    return _dot(tri3, jnp.concatenate(_split3(x), axis=0))


def _fox_cumsum_kernel(lf_ref, c_ref, *, blk):
    s, nh = lf_ref.shape
    tri = _tri3(blk, "lower_incl")

    def body(j, carry):
        r0 = pl.multiple_of(j * blk, blk)
        cs = _tri_sums(tri, lf_ref[pl.ds(r0, blk), :]) + carry
        c_ref[pl.ds(r0, blk), :] = cs
        return cs[blk - 1:blk, :]

    lax.fori_loop(0, s // blk, body, jnp.zeros((1, nh), F32))


def _fox_cumsum(lf, n_seq, seq):
    nh = lf.shape[1]
    blk = ATTN_BLOCK
    assert seq % blk == 0
    return pl.pallas_call(
        functools.partial(_fox_cumsum_kernel, blk=blk),
        grid=(n_seq,),
        in_specs=[pl.BlockSpec((seq, nh), lambda b: (b, 0))],
        out_specs=pl.BlockSpec((seq, nh), lambda b: (b, 0)),
        out_shape=jax.ShapeDtypeStruct((n_seq * seq, nh), F32),
        compiler_params=_params("parallel"),
        name="fox_cumsum",
    )(lf)


def _fox_prompt_kernel(q_ref, k_ref, v_ref, c_ref, ct_ref, o_ref, kb_ref, vt_ref, ckb_ref, m_ref, l_ref, acc_ref, *,
                       grp, hd):
    kv = pl.program_id(1)
    i = pl.program_id(2)
    tq = q_ref.shape[0]
    seq, nh = c_ref.shape
    scale = hd ** -0.5

    @pl.when(i == 0)
    def _():
        kb_ref[...] = k_ref[...].astype(BF16)
        for b in range(0, seq, LANES):
            vt_ref[:, b:b + LANES] = v_ref[b:b + LANES, :].T.astype(BF16)
        parts = _split3(c_ref[...])
        head = lax.broadcasted_iota(jnp.int32, (nh, LANES), 0)
        for g in range(grp):
            sel = jnp.where(head == kv * grp + g, 1.0, 0.0).astype(BF16)
            ckb_ref[g] = _dot(parts[0], sel) + _dot(parts[1], sel) + _dot(parts[2], sel)

    q0 = pl.multiple_of(i * tq, tq)
    cq = jnp.concatenate([ct_ref[0, 0, g:g + 1, pl.ds(q0, tq)] for g in range(grp)], axis=1)
    key_row = lax.broadcasted_iota(jnp.int32, (tq, LANES), 0)
    query = lax.broadcasted_iota(jnp.int32, (tq, LANES), 1)

    m_ref[...] = jnp.full(m_ref.shape, NEG, F32)
    l_ref[...] = jnp.zeros(l_ref.shape, F32)
    acc_ref[...] = jnp.zeros(acc_ref.shape, F32)

    def block(j, masked):
        k0 = pl.multiple_of(j * tq, tq)
        kb = kb_ref[pl.ds(k0, tq), :]
        vtb = vt_ref[:, pl.ds(k0, tq)]
        def scores(c):
            g, r0 = divmod(c, tq)
            return _dot_nt(kb, q_ref[r0:r0 + LANES, g * hd:(g + 1) * hd])

        starts = list(range(0, grp * tq, LANES))
        ahead = [scores(c) for c in starts[:FOX_SCORES_AHEAD]]
        for n, c in enumerate(starts):
            g, r0 = divmod(c, tq)
            cols = slice(c, c + LANES)
            if n + FOX_SCORES_AHEAD < len(starts):
                ahead.append(scores(starts[n + FOX_SCORES_AHEAD]))
            s = ahead[n] * scale + (cq[:, cols] - ckb_ref[g, pl.ds(k0, tq), :])
            if masked:
                s = jnp.where(key_row <= query + r0, s, NEG)
            m_old = m_ref[:, cols]
            m_new = jnp.maximum(m_old, jnp.max(s, axis=0, keepdims=True))
            alpha = jnp.exp(m_old - m_new)
            p = jnp.exp(s - m_new)
            l_ref[:, cols] = alpha * l_ref[:, cols] + jnp.sum(p, axis=0, keepdims=True)
            acc_ref[:, cols] = alpha * acc_ref[:, cols] + _dot(vtb, p.astype(BF16))
            m_ref[:, cols] = m_new

    def body(j, carry):
        block(j, False)
        return carry

    lax.fori_loop(0, i, body, 0)
    block(i, True)
    ot = acc_ref[...] / l_ref[...]
    for g in range(grp):
        o_ref[:, g * hd:(g + 1) * hd] = ot[:, g * tq:(g + 1) * tq].T.astype(o_ref.dtype)


def _fox_prompt(q, k, v, c, ct, n_seq, seq, rows_out, n_kv, hd):
    tq = _pick_tile(seq, FOX_BLOCK_CAP, LANES)
    nb = seq // tq
    nq = q.shape[1]
    nh = c.shape[1]
    grp = nq // hd // n_kv
    assert hd == LANES and tq % LANES == 0 and seq % tq == 0
    return pl.pallas_call(
        functools.partial(_fox_prompt_kernel, grp=grp, hd=hd),
        grid=(n_seq, n_kv, nb),
        in_specs=[pl.BlockSpec((tq, grp * hd), lambda b, kv, i: (b * nb + i, kv)),
                  pl.BlockSpec((seq, hd), lambda b, kv, i: (b, kv)),
                  pl.BlockSpec((seq, hd), lambda b, kv, i: (b, kv)),
                  pl.BlockSpec((seq, nh), lambda b, kv, i: (b, 0)),
                  pl.BlockSpec((1, 1, grp, seq), lambda b, kv, i: (b, kv, 0, 0))],
        out_specs=pl.BlockSpec((tq, grp * hd), lambda b, kv, i: (b * nb + i, kv)),
        out_shape=jax.ShapeDtypeStruct((rows_out, nq), BF16),
        scratch_shapes=[pltpu.VMEM((seq, hd), BF16), pltpu.VMEM((hd, seq), BF16), pltpu.VMEM((grp, seq, LANES), F32),
                        pltpu.VMEM((1, grp * tq), F32), pltpu.VMEM((1, grp * tq), F32),
                        pltpu.VMEM((hd, grp * tq), F32)],
        compiler_params=_params("parallel", "parallel", "arbitrary"),
        name="fox_prompt",
    )(q, k, v, c, ct)


def _fox_sample_kernel(pt_ref, q_ref, kn_ref, vn_ref, lfn_ref, *refs, n_pg, t_new, nh, scale):
    k_refs, v_refs, lf_refs = refs[:n_pg], refs[n_pg:2 * n_pg], refs[2 * n_pg:3 * n_pg]
    o_ref, m_ref, l_ref, acc_ref, run_ref, cn_ref = refs[3 * n_pg:]
    j = pl.program_id(1)
    pg = kn_ref.shape[1]
    ncols = t_new * nh
    assert 2 * ncols == LANES and pg == LANES
    lane = lax.broadcasted_iota(jnp.int32, (pg, LANES), 1)
    row = lax.broadcasted_iota(jnp.int32, (pg, LANES), 0)
    low = lane < ncols
    lane1 = lax.broadcasted_iota(jnp.int32, (1, LANES), 1)
    diag = row == lane
    qb = q_ref[0]

    def widen(lf_a, lf_b):
        return jnp.concatenate([lf_a] * t_new + [lf_b] * t_new, axis=-1)

    def to_rows(x):
        return jnp.sum(jnp.where(diag, x, 0.0), axis=-1, keepdims=True)

    def accumulate(s_tiles, v_pairs):
        m_old = m_ref[...]
        mx = m_old
        for s in s_tiles:
            mx = jnp.maximum(mx, jnp.max(s, axis=0, keepdims=True))
        m_new = jnp.maximum(mx, pltpu.roll(mx, ncols, 1))
        alpha = jnp.exp(m_old - m_new)
        l_new = alpha * l_ref[...]
        acc = to_rows(alpha)[:ncols] * acc_ref[...]
        for s, (va, vb) in zip(s_tiles, v_pairs):
            p = jnp.exp(s - m_new)
            l_new = l_new + jnp.sum(p, axis=0, keepdims=True)
            pt = p.T.astype(BF16)
            acc = acc + _dot(pt[:ncols], va) + _dot(pt[ncols:], vb)
        m_ref[...] = m_new
        l_ref[...] = l_new
        acc_ref[...] = acc

    @pl.when(j == 0)
    def _():
        lfw = widen(lfn_ref[0], lfn_ref[0])
        cnk = _tri_sums(_tri3(pg, "lower_incl"), lfw)
        tok = (lane % ncols) // nh
        cn = jnp.sum(jnp.where(row == tok, cnk, 0.0), axis=0, keepdims=True)
        cn_ref[...] = cn
        run_ref[...] = jnp.zeros(run_ref.shape, F32)
        m_ref[...] = jnp.full(m_ref.shape, NEG, F32)
        l_ref[...] = jnp.zeros(l_ref.shape, F32)
        acc_ref[...] = jnp.zeros(acc_ref.shape, F32)
        s = _dot(kn_ref[0].astype(BF16), qb) * scale + (cn - cnk)
        s = jnp.where((row <= tok) & low, s, NEG)
        vn = vn_ref[0].astype(BF16)
        accumulate([s], [(vn, vn)])

    k3 = lax.broadcasted_iota(jnp.int32, (3 * pg, pg), 0) % pg
    later3 = jnp.where(k3 > lax.broadcasted_iota(jnp.int32, (3 * pg, pg), 1), 1.0, 0.0).astype(BF16)
    col3 = lax.broadcasted_iota(jnp.int32, (LANES, 6 * nh), 0)
    r3 = lax.broadcasted_iota(jnp.int32, (LANES, 6 * nh), 1) % (2 * nh)
    spread3 = jnp.where(r3 == (col3 // ncols) * nh + col3 % nh, 1.0, 0.0).astype(BF16)
    colt = lax.broadcasted_iota(jnp.int32, (2 * nh, LANES), 1)
    spread_t = lax.broadcasted_iota(jnp.int32, (2 * nh, LANES), 0) == (colt // ncols) * nh + colt % nh
    s_tiles, v_pairs = [], []
    run = run_ref[...]
    n_kv = qb.shape[0] // k_refs[0].shape[1]

    def page_rows(ref):
        return jnp.concatenate([ref[pl.ds(kv, pg, stride=n_kv), :] for kv in range(n_kv)], axis=-1).astype(BF16)

    lft = jnp.concatenate([ref[...] for ref in lf_refs], axis=0)
    after_all = _dot(jnp.concatenate(_split3(lft), axis=1), later3)
    page_tot = jnp.sum(lft, axis=1, keepdims=True)
    qk = [_dot(page_rows(ref), qb) for ref in k_refs]
    for a in range(0, n_pg, 2):
        sa, sb = qk[a], qk[a + 1]
        after = after_all[a * nh:(a + 2) * nh]
        after_cols = _dot(spread3, jnp.concatenate(_split3(after), axis=0)).T
        tot = jnp.sum(jnp.where(spread_t, page_tot[a * nh:(a + 2) * nh], 0.0), axis=0, keepdims=True)
        tot_sw = pltpu.roll(tot, ncols, 1)
        bias = after_cols + run + jnp.where(lane1 >= ncols, tot_sw, 0.0) + cn_ref[...]
        s_tiles.append(jnp.where(low, sa, sb) * scale + bias)
        v_pairs.append((page_rows(v_refs[a]), page_rows(v_refs[a + 1])))
        run = run + tot + tot_sw
    run_ref[...] = run
    accumulate(s_tiles, v_pairs)

    @pl.when(j == pl.num_programs(1) - 1)
    def _():
        l_tot = l_ref[...] + pltpu.roll(l_ref[...], ncols, 1)
        o_ref[0] = acc_ref[...] / to_rows(l_tot)[:ncols]


def _fox_sample(q, k_new, v_new, lf_new, cache_k, cache_v, cache_lf, layer, page_table, n_kv, hd):
    n, t, nq = q.shape
    nh = nq // hd
    grp = nh // n_kv
    nk = n_kv * hd
    pg = cache_k.shape[2]
    n_pages = page_table.shape[1]
    ncols = t * nh
    n_pg = _pick_tile(n_pages, FOX_PAGES_PER_STEP, 2)
    assert n_pages % n_pg == 0 and n_pg % 2 == 0
    qt = q.reshape(n, t, n_kv, grp, hd).transpose(0, 4, 1, 2, 3)
    own = jnp.eye(n_kv, dtype=bool)[None, :, None, None, :, None]
    qb = jnp.where(own, qt[:, None], jnp.zeros((), q.dtype)).reshape(n, nk, ncols)
    qb = jnp.concatenate([qb, qb], axis=-1)
    padrows = lambda a: jnp.pad(a, ((0, 0), (0, pg - t), (0, 0)))

    rows2d = lambda c: c.reshape(c.shape[0], c.shape[1], pg * n_kv, hd)

    def page_id(i, j, pt, p):
        return pt[i, n_pages - 1 - (j * n_pg + p)]

    def kv_page(p):
        return pl.BlockSpec((None, None, pg * n_kv, hd), lambda i, j, pt: (layer, page_id(i, j, pt, p), 0, 0))

    def lf_page(p):
        return pl.BlockSpec((None, None, nh, pg), lambda i, j, pt: (layer, page_id(i, j, pt, p), 0, 0))

    per_seq = lambda a, b: pl.BlockSpec((1, a, b), lambda i, j, pt: (i, 0, 0))
    o = pl.pallas_call(
        functools.partial(_fox_sample_kernel, n_pg=n_pg, t_new=t, nh=nh, scale=hd ** -0.5),
        grid_spec=pltpu.PrefetchScalarGridSpec(
            num_scalar_prefetch=1,
            grid=(n, n_pages // n_pg),
            in_specs=[per_seq(nk, 2 * ncols), per_seq(pg, nk), per_seq(pg, nk), per_seq(pg, nh)]
            + [kv_page(p) for p in range(n_pg)] + [kv_page(p) for p in range(n_pg)]
            + [lf_page(p) for p in range(n_pg)],
            out_specs=per_seq(ncols, nk),
            scratch_shapes=[pltpu.VMEM((1, LANES), F32), pltpu.VMEM((1, LANES), F32), pltpu.VMEM((ncols, nk), F32),
                            pltpu.VMEM((1, LANES), F32), pltpu.VMEM((1, LANES), F32)]),
        out_shape=jax.ShapeDtypeStruct((n, ncols, nk), F32),
        compiler_params=_params("parallel", "arbitrary"),
        name="fox_sample",
    )(page_table, qb, padrows(k_new), padrows(v_new), padrows(lf_new), *([rows2d(cache_k)] * n_pg),
      *([rows2d(cache_v)] * n_pg),
      *([jnp.swapaxes(cache_lf, 2, 3)] * n_pg))
    o = o.reshape(n, t, n_kv, grp, n_kv, hd)
    return jnp.stack([o[:, :, kv, :, kv, :] for kv in range(n_kv)], axis=2).reshape(n, t, nq)


def _rope_tables(pos, hd):
    half = hd // 2
    inv = ROPE_THETA ** (-2.0 * jnp.arange(half, dtype=F32) / hd)
    ang = pos.astype(F32)[:, None] * inv[None, :]
    cos = jnp.concatenate([jnp.cos(ang), jnp.cos(ang)], axis=-1)
    sin = jnp.concatenate([-jnp.sin(ang), jnp.sin(ang)], axis=-1)
    reps = LANES // hd
    return jnp.tile(cos, (1, reps)), jnp.tile(sin, (1, reps))


def kernel(x_prompt, x_sample, p_prompt, p_sample, cache_conv, cache_swa_k, cache_swa_v, cache_fox_k, cache_fox_v, cache_fox_logf, page_table, norm_mix, norm_ffn, norm_ple, norm_final, conv_w_in, conv_w_dw, conv_b_dw, conv_ln_g, conv_ln_b, conv_w_out, swa_w_qkv, swa_sinks, swa_w_o, fox_w_in, fox_b_f, fox_w_o, ffn_w_gate, ffn_w_up, ffn_w_down, ple_w_proj, ple_w_gate):
    nb, seq, d = x_prompt.shape
    ns, ts, _ = x_sample.shape
    depth = norm_mix.shape[0]
    mp, ms = nb * seq, ns * ts
    m = mp + ms
    tm = _pick_tile(m, ROW_TILE_CAP, BF16_SUBLANES)
    kw = conv_w_dw.shape[1]
    window, swa_kv, swa_hd = cache_swa_k.shape[2:]
    swa_nq, swa_nk = swa_sinks.shape[1] * swa_hd, swa_kv * swa_hd
    fox_kv, fox_hd = cache_fox_k.shape[3:]
    fox_h = fox_b_f.shape[1]
    fox_nq, fox_nk = fox_h * fox_hd, fox_kv * fox_hd
    past_len = page_table.shape[1] * cache_fox_k.shape[2]

    x = jnp.concatenate([x_prompt.reshape(mp, d), x_sample.reshape(ms, d)], axis=0)
    pos = jnp.concatenate([jnp.tile(jnp.arange(seq, dtype=jnp.int32), nb),
                           jnp.tile(past_len + jnp.arange(ts, dtype=jnp.int32), ns)])
    cos, sin = _rope_tables(pos, swa_hd)
    vec = lambda a: a.reshape(1, -1)
    bf16 = lambda w: w.astype(BF16)
    conv_w_in, conv_w_out, swa_w_qkv, swa_w_o, fox_w_in, fox_w_o = map(
        bf16, (conv_w_in, conv_w_out, swa_w_qkv, swa_w_o, fox_w_in, fox_w_o))
    ple_w_proj, ple_w_gate = bf16(ple_w_proj), bf16(ple_w_gate)
    p_p = p_prompt.reshape(depth, mp, -1)
    p_s = p_sample.reshape(depth, ms, -1)

    conv_p, conv_s = [], []
    swk_p, swv_p, swk_s, swv_s = [], [], [], []
    fk_p, fv_p, fl_p, fk_s, fv_s, fl_s = [], [], [], [], [], []
    for i in range(depth):
        kind, j = i % 3, i // 3
        g_mix = vec(norm_mix[i])
        if kind == 0:
            u = _conv_in(x, g_mix, conv_w_in, j, tm)
            dc = u.shape[1]
            conv_w = (conv_w_dw[j], vec(conv_b_dw[j]), vec(conv_ln_g[j]), vec(conv_ln_b[j]))
            o_p, c_p = _conv_core(u, jnp.zeros((nb, kw - 1, dc), F32), *conv_w, nb, seq, mp)
            o_s, c_s = _conv_core(u[mp:].reshape(ns, ts, dc), cache_conv[j], *conv_w, ns, ts, None)
            conv_p.append(c_p)
            conv_s.append(c_s)
            w_o = conv_w_out
        elif kind == 1:
            q, k, v = _swa_in(x, g_mix, swa_w_qkv, j, cos, sin, tm, swa_nq, swa_nk, swa_hd)
            o_p = _swa_prompt(q, k, v, swa_sinks[j], nb, seq, mp, swa_kv, swa_hd, window)
            o_s, ks_, vs_ = _swa_sample(q[mp:].reshape(ns, ts, swa_nq), k[mp:].reshape(ns, ts, swa_nk),
                                        v[mp:].reshape(ns, ts, swa_nk), cache_swa_k[j].reshape(ns, window, swa_nk),
                                        cache_swa_v[j].reshape(ns, window, swa_nk), swa_sinks[j], swa_kv, swa_hd,
                                        window, past_len)
            k_p, v_p = k[:mp].reshape(nb, seq, swa_kv, swa_hd), v[:mp].reshape(nb, seq, swa_kv, swa_hd)
            swk_p.append(k_p[:, seq - window:])
            swv_p.append(v_p[:, seq - window:])
            swk_s.append(ks_.reshape(ns, window, swa_kv, swa_hd))
            swv_s.append(vs_.reshape(ns, window, swa_kv, swa_hd))
            w_o = swa_w_o
        else:
            q, k, v, lf = _fox_in(x, g_mix, fox_w_in, j, vec(fox_b_f[j]), tm, fox_nq, fox_nk)
            c = _fox_cumsum(lf, nb, seq)
            ct = c.reshape(nb, seq, fox_kv, fox_h // fox_kv).transpose(0, 2, 3, 1)
            o_p = _fox_prompt(q, k, v, c, ct, nb, seq, mp, fox_kv, fox_hd)
            o_s = _fox_sample(q[mp:].reshape(ns, ts, fox_nq), k[mp:].reshape(ns, ts, fox_nk),
                              v[mp:].reshape(ns, ts, fox_nk), lf[mp:].reshape(ns, ts, fox_h),
                              cache_fox_k, cache_fox_v, cache_fox_logf, j, page_table, fox_kv, fox_hd)
            fk_p.append(k[:mp].reshape(nb, seq, fox_kv, fox_hd))
            fv_p.append(v[:mp].reshape(nb, seq, fox_kv, fox_hd))
            fl_p.append(lf[:mp].reshape(nb, seq, fox_h))
            fk_s.append(k[mp:].reshape(ns, ts, fox_kv, fox_hd))
            fv_s.append(v[mp:].reshape(ns, ts, fox_kv, fox_hd))
            fl_s.append(lf[mp:].reshape(ns, ts, fox_h))
            w_o = fox_w_o
        x = _out_proj(o_p, o_s.reshape(ms, -1).astype(BF16), w_o, j, x, tm)
        x = _ffn(x, vec(norm_ffn[i]), ffn_w_gate, ffn_w_up, ffn_w_down, i)
        x = _ple(x, p_p, p_s, vec(norm_ple[i]), ple_w_proj, ple_w_gate, i, vec(norm_final), tm,
                 final=(i == depth - 1))

    return (x[:mp].reshape(nb, seq, d), x[mp:].reshape(ns, ts, d),
            jnp.stack(conv_p), jnp.stack(conv_s),
            jnp.stack(swk_p), jnp.stack(swv_p), jnp.stack(swk_s), jnp.stack(swv_s),
            jnp.stack(fk_p), jnp.stack(fv_p), jnp.stack(fl_p),
            jnp.stack(fk_s), jnp.stack(fv_s), jnp.stack(fl_s))
```

```python
import functools

import jax
import jax.numpy as jnp
from jax import lax
from jax.experimental import pallas as pl
from jax.experimental.pallas import tpu as pltpu

EPS = 1e-6
NEG = -1e30
ROPE_THETA = 10000.0
F32 = jnp.float32
BF16 = jnp.bfloat16

LANES = 128
SUBLANES = 8
BF16_SUBLANES = 16
VMEM_LIMIT = 60 * 1024 * 1024
ROW_TILE_CAP = 640
FFN_ROW_TILE_CAP = 1040
FFN_COL_TILE_CAP = 256
FFN_WEIGHT_PARTS = 4
COL_CHUNK = 512
ATTN_BLOCK = 128
FOX_BLOCK_CAP = 256
FOX_SCORES_AHEAD = 4
CONV_TIME_TILE_CAP = 128
CONV_ROW_CHUNK = 32
FOX_PAGES_PER_STEP = 32


def _pick_tile(n, cap, mult):
    best = None
    for d in range(mult, min(n, cap) + 1, mult):
        if n % d == 0:
            best = d
    return best if best is not None else n


def _params(*sem):
    return pltpu.CompilerParams(dimension_semantics=sem, vmem_limit_bytes=VMEM_LIMIT)


def _resident(shape):
    nd = len(shape)
    return pl.BlockSpec(shape, lambda *_: (0,) * nd, pipeline_mode=pl.Buffered(1))


def _layer_resident(w, layer):
    return pl.BlockSpec((None,) + w.shape[1:], lambda *_: (layer, 0, 0), pipeline_mode=pl.Buffered(1))


def _rmsnorm(x, g):
    return x * lax.rsqrt(jnp.mean(x * x, axis=-1, keepdims=True) + EPS) * g


def _dot(a, b):
    return jnp.dot(a, b, preferred_element_type=F32)


def _dot_nt(a, b):
    return lax.dot_general(a, b, (((1,), (1,)), ((), ())), preferred_element_type=F32)


def _col_chunks(n, chunk=COL_CHUNK):
    return [(s, min(chunk, n - s)) for s in range(0, n, chunk)]


def _split3(x):
    hi = x.astype(BF16)
    r1 = x - hi.astype(F32)
    mid = r1.astype(BF16)
    lo = (r1 - mid.astype(F32)).astype(BF16)
    return hi, mid, lo


def _conv_in_kernel(x_ref, g_ref, w_ref, u_ref):
    h = _rmsnorm(x_ref[...], g_ref[...]).astype(BF16)
    dc = u_ref.shape[-1]
    for s, n in _col_chunks(dc):
        a = _dot(h, w_ref[:, s:s + n])
        b = _dot(h, w_ref[:, dc + s:dc + s + n])
        u_ref[:, s:s + n] = a * jax.nn.sigmoid(b)


def _conv_in(x, g, w, layer, tm):
    m, d = x.shape
    dc = w.shape[2] // 2
    return pl.pallas_call(
        _conv_in_kernel,
        grid=(m // tm,),
        in_specs=[pl.BlockSpec((tm, d), lambda i: (i, 0)), _resident((1, d)), _layer_resident(w, layer)],
        out_specs=pl.BlockSpec((tm, dc), lambda i: (i, 0)),
        out_shape=jax.ShapeDtypeStruct((m, dc), F32),
        compiler_params=_params("parallel"),
        name="conv_in",
    )(x, g, w)


def _swa_in_kernel(x_ref, g_ref, w_ref, cos_ref, sin_ref, q_ref, k_ref, v_ref, *, hd):
    h = _rmsnorm(x_ref[...], g_ref[...]).astype(BF16)
    cos = cos_ref[...]
    sin = sin_ref[...]
    lane = lax.broadcasted_iota(jnp.int32, cos.shape, 1)
    first_half = (lane % hd) < hd // 2

    def rope(z):
        partner = jnp.where(first_half, pltpu.roll(z, LANES - hd // 2, 1), pltpu.roll(z, hd // 2, 1))
        return z * cos + partner * sin

    nq, nk = q_ref.shape[-1], k_ref.shape[-1]
    for s, n in _col_chunks(nq):
        z = _dot(h, w_ref[:, s:s + n])
        for c in range(0, n, LANES):
            q_ref[:, s + c:s + c + LANES] = rope(z[:, c:c + LANES]).astype(q_ref.dtype)
    for s, n in _col_chunks(nk):
        z = _dot(h, w_ref[:, nq + s:nq + s + n])
        for c in range(0, n, LANES):
            k_ref[:, s + c:s + c + LANES] = rope(z[:, c:c + LANES])
    v_ref[...] = _dot(h, w_ref[:, nq + nk:])


def _swa_in(x, g, w, layer, cos, sin, tm, nq, nk, hd):
    m, d = x.shape
    assert LANES % hd == 0 and nq % LANES == 0 and nk % LANES == 0
    row = lambda n: pl.BlockSpec((tm, n), lambda i: (i, 0))
    return pl.pallas_call(
        functools.partial(_swa_in_kernel, hd=hd),
        grid=(m // tm,),
        in_specs=[row(d), _resident((1, d)), _layer_resident(w, layer), row(LANES), row(LANES)],
        out_specs=[row(nq), row(nk), row(nk)],
        out_shape=[jax.ShapeDtypeStruct((m, nq), BF16), jax.ShapeDtypeStruct((m, nk), F32),
                   jax.ShapeDtypeStruct((m, nk), F32)],
        compiler_params=_params("parallel"),
        name="swa_in",
    )(x, g, w, cos, sin)


def _fox_in_kernel(x_ref, g_ref, w_ref, bf_ref, q_ref, k_ref, v_ref, lf_ref):
    h = _rmsnorm(x_ref[...], g_ref[...]).astype(BF16)
    nq, nk = q_ref.shape[-1], k_ref.shape[-1]
    for s, n in _col_chunks(nq):
        q_ref[:, s:s + n] = _dot(h, w_ref[:, s:s + n]).astype(q_ref.dtype)
    for s, n in _col_chunks(nk):
        k_ref[:, s:s + n] = _dot(h, w_ref[:, nq + s:nq + s + n])
        v_ref[:, s:s + n] = _dot(h, w_ref[:, nq + nk + s:nq + nk + s + n])
    lf_ref[...] = jax.nn.log_sigmoid(_dot(h, w_ref[:, nq + 2 * nk:]) + bf_ref[...])


def _fox_in(x, g, w, layer, bf, tm, nq, nk):
    m, d = x.shape
    nh = w.shape[2] - nq - 2 * nk
    row = lambda n: pl.BlockSpec((tm, n), lambda i: (i, 0))
    return pl.pallas_call(
        _fox_in_kernel,
        grid=(m // tm,),
        in_specs=[row(d), _resident((1, d)), _layer_resident(w, layer), _resident((1, nh))],
        out_specs=[row(nq), row(nk), row(nk), row(nh)],
        out_shape=[jax.ShapeDtypeStruct((m, nq), BF16), jax.ShapeDtypeStruct((m, nk), F32),
                   jax.ShapeDtypeStruct((m, nk), F32), jax.ShapeDtypeStruct((m, nh), F32)],
        compiler_params=_params("parallel"),
        name="fox_in",
    )(x, g, w, bf)


def _out_proj_kernel(ap_ref, as_ref, w_ref, x_ref, o_ref):
    tm = x_ref.shape[0]
    ms = as_ref.shape[0]
    last = pl.num_programs(0) - 1

    def project(a):
        for s, n in _col_chunks(o_ref.shape[-1]):
            o_ref[:, s:s + n] = x_ref[:, s:s + n] + _dot(a, w_ref[:, s:s + n])

    @pl.when(pl.program_id(0) < last)
    def _():
        project(ap_ref[...])

    @pl.when(pl.program_id(0) == last)
    def _():
        project(jnp.concatenate([ap_ref[:tm - ms, :], as_ref[...]], axis=0))


def _out_proj(a_p, a_s, w, layer, x, tm):
    m, d = x.shape
    k = a_p.shape[1]
    ms = a_s.shape[0]
    assert a_p.shape[0] + ms == m and ms <= tm and (tm - ms) % BF16_SUBLANES == 0
    return pl.pallas_call(
        _out_proj_kernel,
        grid=(m // tm,),
        in_specs=[pl.BlockSpec((tm, k), lambda i: (i, 0)), _resident((ms, k)), _layer_resident(w, layer),
                  pl.BlockSpec((tm, d), lambda i: (i, 0))],
        out_specs=pl.BlockSpec((tm, d), lambda i: (i, 0)),
        out_shape=jax.ShapeDtypeStruct((m, d), F32),
        compiler_params=_params("parallel"),
        name="out_proj",
    )(a_p, a_s, w, x)


def _ffn_kernel(x_ref, g_ref, *refs, parts):
    wg_refs, wu_refs, wd_refs = refs[:parts], refs[parts:2 * parts], refs[2 * parts:3 * parts]
    o_ref, h_ref = refs[3 * parts:]

    @pl.when(pl.program_id(1) == 0)
    def _():
        x = x_ref[...]
        h_ref[...] = _rmsnorm(x, g_ref[...]).astype(BF16)
        o_ref[...] = x

    dk = h_ref.shape[1] // parts
    cast = lambda w_ref: w_ref[...].astype(BF16)
    up_proj = lambda w_refs: sum(_dot(h_ref[:, q * dk:(q + 1) * dk], cast(w_refs[q])) for q in range(parts))
    a = (jax.nn.silu(up_proj(wg_refs)) * up_proj(wu_refs)).astype(BF16)
    for q in range(parts):
        o_ref[:, q * dk:(q + 1) * dk] += _dot(a, cast(wd_refs[q]))


def _ffn(x, g, wg, wu, wd, layer):
    m, d = x.shape
    f = wg.shape[2]
    tm = _pick_tile(m, FFN_ROW_TILE_CAP, BF16_SUBLANES)
    tf = _pick_tile(f, FFN_COL_TILE_CAP, LANES)
    parts = FFN_WEIGHT_PARTS
    dk = d // parts
    assert d % (parts * LANES) == 0
    up_spec = lambda q: pl.BlockSpec((None, dk, tf), lambda i, j: (layer, q, j))
    down_spec = lambda q: pl.BlockSpec((None, tf, dk), lambda i, j: (layer, j, q))
    return pl.pallas_call(
        functools.partial(_ffn_kernel, parts=parts),
        grid=(m // tm, f // tf),
        in_specs=[pl.BlockSpec((tm, d), lambda i, j: (i, 0)), pl.BlockSpec((1, d), lambda i, j: (0, 0))]
        + [up_spec(q) for q in range(parts)] + [up_spec(q) for q in range(parts)]
        + [down_spec(q) for q in range(parts)],
        out_specs=pl.BlockSpec((tm, d), lambda i, j: (i, 0)),
        out_shape=jax.ShapeDtypeStruct((m, d), F32),
        scratch_shapes=[pltpu.VMEM((tm, d), BF16)],
        compiler_params=_params("parallel", "arbitrary"),
        name="ffn",
    )(x, g, *([wg] * parts), *([wu] * parts), *([wd] * parts))


def _ple_kernel(x_ref, pp_ref, ps_ref, g_ref, wp_ref, wg_ref, gf_ref, o_ref, p_ref, *, final):
    tm = x_ref.shape[0]
    ms = ps_ref.shape[0]
    last = pl.num_programs(0) - 1

    @pl.when(pl.program_id(0) < last)
    def _():
        p_ref[...] = pp_ref[...].astype(BF16)

    @pl.when(pl.program_id(0) == last)
    def _():
        p_ref[:tm - ms, :] = pp_ref[:tm - ms, :].astype(BF16)
        p_ref[tm - ms:, :] = ps_ref[...].astype(BF16)

    h = _rmsnorm(x_ref[...], g_ref[...]).astype(BF16)
    p = p_ref[...]
    for s, n in _col_chunks(o_ref.shape[-1]):
        gate = jax.nn.sigmoid(_dot(h, wg_ref[:, s:s + n]))
        o_ref[:, s:s + n] = x_ref[:, s:s + n] + _dot(p, wp_ref[:, s:s + n]) * gate
    if final:
        o_ref[...] = _rmsnorm(o_ref[...], gf_ref[...])


def _ple(x, p_p, p_s, g, wp, wg, layer, gf, tm, final):
    m, d = x.shape
    dp = p_p.shape[2]
    ms = p_s.shape[1]
    assert p_p.shape[1] + ms == m and ms <= tm and (tm - ms) % BF16_SUBLANES == 0
    return pl.pallas_call(
        functools.partial(_ple_kernel, final=final),
        grid=(m // tm,),
        in_specs=[pl.BlockSpec((tm, d), lambda i: (i, 0)), pl.BlockSpec((None, tm, dp), lambda i: (layer, i, 0)),
                  _layer_resident(p_s, layer), _resident((1, d)), _layer_resident(wp, layer),
                  _layer_resident(wg, layer), _resident((1, d))],
        out_specs=pl.BlockSpec((tm, d), lambda i: (i, 0)),
        out_shape=jax.ShapeDtypeStruct((m, d), F32),
        scratch_shapes=[pltpu.VMEM((tm, dp), BF16)],
        compiler_params=_params("parallel"),
        name="ple_final" if final else "ple",
    )(x, p_p, p_s, g, wp, wg, gf)


def _conv_kernel(u_ref, hist_ref, w_ref, b_ref, lg_ref, lb_ref, y_ref, cache_ref, ext_ref, sh_ref, wb_ref, conv_ref,
                 *, tt, kw, pad):
    t = pl.program_id(1)
    off = pad - (kw - 1)

    sub = wb_ref.shape[1]

    @pl.when(t == 0)
    def _():
        ext_ref[off:pad, :] = hist_ref[0]
        for k in range(kw):
            wb_ref[k] = jnp.broadcast_to(w_ref[k:k + 1, :], wb_ref.shape[1:])

    @pl.when(t > 0)
    def _():
        ext_ref[off:pad, :] = ext_ref[tt + off:tt + pad, :]

    ext_ref[pad:pad + tt, :] = u_ref[...].reshape(tt, u_ref.shape[-1])
    cache_ref[0] = ext_ref[tt + off:tt + pad, :]

    d = ext_ref.shape[-1]
    rc = min(CONV_ROW_CHUNK, tt)
    rows, cn = sh_ref.shape[1:]
    for cs in range(0, d, cn):
        for b in range(1, SUBLANES):
            sh_ref[b - 1] = ext_ref[b:b + rows, cs:cs + cn]
        for r in range(0, tt, rc):
            accs = [jnp.broadcast_to(b_ref[:, cs:cs + cn], (sub, cn))] * (rc // sub)
            for k in range(kw):
                a, b = divmod(off + k, SUBLANES)
                wk = wb_ref[k, :, cs:cs + cn]
                for q in range(rc // sub):
                    r0 = a * SUBLANES + r + q * sub
                    src = ext_ref[r0:r0 + sub, cs:cs + cn] if b == 0 else sh_ref[b - 1, r0:r0 + sub, :]
                    accs[q] = accs[q] + src * wk
            for q in range(rc // sub):
                conv_ref[r + q * sub:r + (q + 1) * sub, cs:cs + cn] = accs[q]
    y = conv_ref[...]
    mu = jnp.mean(y, axis=-1, keepdims=True)
    yc = y - mu
    var = jnp.mean(yc * yc, axis=-1, keepdims=True)
    z = yc * lax.rsqrt(var + EPS) * lg_ref[...] + lb_ref[...]
    y_ref[...] = jax.nn.silu(z).astype(y_ref.dtype).reshape(y_ref.shape)


def _conv_core(u, hist, w_dw, b_dw, ln_g, ln_b, n, t, rows_out):
    d = u.shape[-1]
    kw = w_dw.shape[0]
    pad = -(-(kw - 1) // SUBLANES) * SUBLANES
    if u.ndim == 2:
        tt = _pick_tile(t, CONV_TIME_TILE_CAP, SUBLANES)
        nt = t // tt
        row_spec = pl.BlockSpec((tt, d), lambda i, j: (i * nt + j, 0))
        y_shape = (rows_out, d)
    else:
        tt = t
        row_spec = pl.BlockSpec((1, t, d), lambda i, j: (i, 0, 0))
        y_shape = (n, t, d)
    assert tt == t or tt >= kw - 1
    assert d % COL_CHUNK == 0
    vec = lambda: pl.BlockSpec((1, d), lambda i, j: (0, 0))
    return pl.pallas_call(
        functools.partial(_conv_kernel, tt=tt, kw=kw, pad=pad),
        grid=(n, t // tt),
        in_specs=[row_spec, pl.BlockSpec((1, kw - 1, d), lambda i, j: (i, 0, 0)),
                  pl.BlockSpec((kw, d), lambda i, j: (0, 0)), vec(), vec(), vec()],
        out_specs=[row_spec, pl.BlockSpec((1, kw - 1, d), lambda i, j: (i, 0, 0))],
        out_shape=[jax.ShapeDtypeStruct(y_shape, BF16), jax.ShapeDtypeStruct((n, kw - 1, d), F32)],
        scratch_shapes=[pltpu.VMEM((pad + tt, d), F32),
                        pltpu.VMEM((SUBLANES - 1, tt + pad - SUBLANES, COL_CHUNK), F32),
                        pltpu.VMEM((kw, min(SUBLANES, tt), d), F32), pltpu.VMEM((tt, d), F32)],
        compiler_params=_params("parallel", "arbitrary"),
        name="conv_core",
    )(u, hist, w_dw, b_dw, ln_g, ln_b)


def _swa_prompt_kernel(sink_ref, q_ref, kp_ref, kc_ref, vp_ref, vc_ref, o_ref, *, n_heads, n_kv, hd, window):
    i = pl.program_id(1)
    tq = q_ref.shape[0]
    grp = n_heads // n_kv
    key = lax.broadcasted_iota(jnp.int32, (2 * tq, grp * tq), 0)
    qry = lax.broadcasted_iota(jnp.int32, (2 * tq, grp * tq), 1) % tq
    dist = qry + tq - key
    valid = (dist >= 0) & (dist <= window) & ((key >= tq) | (i > 0))
    scale = hd ** -0.5
    kcat = jnp.concatenate([kp_ref[...], kc_ref[...]], axis=0).astype(BF16)
    vt = jnp.concatenate([vp_ref[...], vc_ref[...]], axis=0).T.astype(BF16)
    per_tile = LANES // hd
    for kv in range(n_kv):
        heads = range(kv * grp, (kv + 1) * grp)
        qs = jnp.concatenate([q_ref[:, h * hd:(h + 1) * hd] for h in heads], axis=0)
        sink = jnp.concatenate([jnp.full((1, tq), sink_ref[h], F32) for h in heads], axis=1)
        s = _dot_nt(kcat[:, kv * hd:(kv + 1) * hd], qs) * scale
        s = jnp.where(valid, s, NEG)
        m = jnp.maximum(jnp.max(s, axis=0, keepdims=True), sink)
        e = jnp.exp(s - m)
        den = jnp.sum(e, axis=0, keepdims=True) + jnp.exp(sink - m)
        ot = _dot(vt[kv * hd:(kv + 1) * hd, :], e.astype(BF16)) / den
        for g in range(0, grp, per_tile):
            tile = jnp.concatenate([ot[:, (g + t) * tq:(g + t + 1) * tq] for t in range(per_tile)], axis=0)
            c0 = (kv * grp + g) * hd
            o_ref[:, c0:c0 + LANES] = tile.T.astype(o_ref.dtype)


def _swa_prompt(q, k, v, sinks, n_seq, seq, rows_out, n_kv, hd, window):
    tq = ATTN_BLOCK
    assert seq % tq == 0 and window <= tq
    nb = seq // tq
    nq, nk = q.shape[1], k.shape[1]
    cur = lambda n: pl.BlockSpec((tq, n), lambda b, i: (b * nb + i, 0))
    prev = lambda n: pl.BlockSpec((tq, n), lambda b, i: (jnp.maximum(b * nb + i - 1, 0), 0))
    return pl.pallas_call(
        functools.partial(_swa_prompt_kernel, n_heads=nq // hd, n_kv=n_kv, hd=hd, window=window),
        grid=(n_seq, nb),
        in_specs=[pl.BlockSpec(memory_space=pltpu.SMEM), cur(nq), prev(nk), cur(nk), prev(nk), cur(nk)],
        out_specs=cur(nq),
        out_shape=jax.ShapeDtypeStruct((rows_out, nq), BF16),
        compiler_params=_params("parallel", "arbitrary"),
        name="swa_prompt",
    )(sinks, q, k, k, v, v)


def _swa_sample_kernel(q_ref, sink_ref, kc_ref, kn_ref, vc_ref, vn_ref, o_ref, ko_ref, vo_ref, *, hd, window, t_new,
                       past_len):
    nseq, n_kv, rows, _ = q_ref.shape
    nbuf = kc_ref.shape[1]
    ts = nbuf + t_new
    r = lax.broadcasted_iota(jnp.int32, (rows, ts), 0)
    c = lax.broadcasted_iota(jnp.int32, (rows, ts), 1)
    dist = (r % t_new) + nbuf - c
    valid = (dist >= 0) & (dist <= window) & (c >= nbuf - past_len)
    scale = hd ** -0.5
    for n in range(nseq):
        kall = jnp.concatenate([kc_ref[n], kn_ref[n]], axis=0)
        vall = jnp.concatenate([vc_ref[n], vn_ref[n]], axis=0)
        ko_ref[n] = kall[ts - nbuf:]
        vo_ref[n] = vall[ts - nbuf:]
        kb = kall.astype(BF16)
        vb = vall.astype(BF16)
        for kv in range(n_kv):
            s = _dot_nt(q_ref[n, kv], kb[:, kv * hd:(kv + 1) * hd]) * scale
            s = jnp.where(valid, s, NEG)
            sink = sink_ref[kv]
            m = jnp.maximum(jnp.max(s, axis=-1, keepdims=True), sink)
            e = jnp.exp(s - m)
            den = jnp.sum(e, axis=-1, keepdims=True) + jnp.exp(sink - m)
            p = (e / den).astype(BF16)
            o_ref[n, kv] = _dot(p, vb[:, kv * hd:(kv + 1) * hd]).astype(o_ref.dtype)


def _swa_sample(q, k_new, v_new, k_cache, v_cache, sinks, n_kv, hd, window, past_len):
    n, t, nq = q.shape
    grp = nq // hd // n_kv
    nbuf = k_cache.shape[1]
    nk = n_kv * hd
    rows = grp * t
    sb = _pick_tile(n, 8, 1)
    qr = q.reshape(n, t, n_kv, grp, hd).transpose(0, 2, 3, 1, 4).reshape(n, n_kv, rows, hd)
    sink_rows = jnp.broadcast_to(sinks.reshape(n_kv, grp, 1, 1), (n_kv, grp, t, 1)).reshape(n_kv, rows, 1)
    seq3 = lambda a, b: pl.BlockSpec((sb, a, b), lambda i: (i, 0, 0))
    seq4 = pl.BlockSpec((sb, n_kv, rows, hd), lambda i: (i, 0, 0, 0))
    o, ko, vo = pl.pallas_call(
        functools.partial(_swa_sample_kernel, hd=hd, window=window, t_new=t, past_len=past_len),
        grid=(n // sb,),
        in_specs=[seq4, _resident((n_kv, rows, 1)), seq3(nbuf, nk), seq3(t, nk), seq3(nbuf, nk), seq3(t, nk)],
        out_specs=[seq4, seq3(nbuf, nk), seq3(nbuf, nk)],
        out_shape=[jax.ShapeDtypeStruct((n, n_kv, rows, hd), BF16), jax.ShapeDtypeStruct((n, nbuf, nk), F32),
                   jax.ShapeDtypeStruct((n, nbuf, nk), F32)],
        compiler_params=_params("parallel"),
        name="swa_sample",
    )(qr, sink_rows, k_cache, k_new, v_cache, v_new)
    o = o.reshape(n, n_kv, grp, t, hd).transpose(0, 3, 1, 2, 4).reshape(n, t, nq)
    return o, ko, vo


def _tri3(n, kind):
    r = lax.broadcasted_iota(jnp.int32, (n, 3 * n), 0)
    c = lax.broadcasted_iota(jnp.int32, (n, 3 * n), 1) % n
    keep = (c <= r) if kind == "lower_incl" else (c > r)
    return jnp.where(keep, 1.0, 0.0).astype(BF16)


def _tri_sums(tri3, x):
    return _dot(tri3, jnp.concatenate(_split3(x), axis=0))


def _fox_cumsum_kernel(lf_ref, c_ref, *, blk):
    s, nh = lf_ref.shape
    tri = _tri3(blk, "lower_incl")

    def body(j, carry):
        r0 = pl.multiple_of(j * blk, blk)
        cs = _tri_sums(tri, lf_ref[pl.ds(r0, blk), :]) + carry
        c_ref[pl.ds(r0, blk), :] = cs
        return cs[blk - 1:blk, :]

    lax.fori_loop(0, s // blk, body, jnp.zeros((1, nh), F32))


def _fox_cumsum(lf, n_seq, seq):
    nh = lf.shape[1]
    blk = ATTN_BLOCK
    assert seq % blk == 0
    return pl.pallas_call(
        functools.partial(_fox_cumsum_kernel, blk=blk),
        grid=(n_seq,),
        in_specs=[pl.BlockSpec((seq, nh), lambda b: (b, 0))],
        out_specs=pl.BlockSpec((seq, nh), lambda b: (b, 0)),
        out_shape=jax.ShapeDtypeStruct((n_seq * seq, nh), F32),
        compiler_params=_params("parallel"),
        name="fox_cumsum",
    )(lf)


def _fox_prompt_kernel(q_ref, k_ref, v_ref, c_ref, ct_ref, o_ref, kb_ref, vt_ref, ckb_ref, m_ref, l_ref, acc_ref, *,
                       grp, hd):
    kv = pl.program_id(1)
    i = pl.program_id(2)
    tq = q_ref.shape[0]
    seq, nh = c_ref.shape
    scale = hd ** -0.5

    @pl.when(i == 0)
    def _():
        kb_ref[...] = k_ref[...].astype(BF16)
        for b in range(0, seq, LANES):
            vt_ref[:, b:b + LANES] = v_ref[b:b + LANES, :].T.astype(BF16)
        parts = _split3(c_ref[...])
        head = lax.broadcasted_iota(jnp.int32, (nh, LANES), 0)
        for g in range(grp):
            sel = jnp.where(head == kv * grp + g, 1.0, 0.0).astype(BF16)
            ckb_ref[g] = _dot(parts[0], sel) + _dot(parts[1], sel) + _dot(parts[2], sel)

    q0 = pl.multiple_of(i * tq, tq)
    cq = jnp.concatenate([ct_ref[0, 0, g:g + 1, pl.ds(q0, tq)] for g in range(grp)], axis=1)
    key_row = lax.broadcasted_iota(jnp.int32, (tq, LANES), 0)
    query = lax.broadcasted_iota(jnp.int32, (tq, LANES), 1)

    m_ref[...] = jnp.full(m_ref.shape, NEG, F32)
    l_ref[...] = jnp.zeros(l_ref.shape, F32)
    acc_ref[...] = jnp.zeros(acc_ref.shape, F32)

    def block(j, masked):
        k0 = pl.multiple_of(j * tq, tq)
        kb = kb_ref[pl.ds(k0, tq), :]
        vtb = vt_ref[:, pl.ds(k0, tq)]
        def scores(c):
            g, r0 = divmod(c, tq)
            return _dot_nt(kb, q_ref[r0:r0 + LANES, g * hd:(g + 1) * hd])

        starts = list(range(0, grp * tq, LANES))
        ahead = [scores(c) for c in starts[:FOX_SCORES_AHEAD]]
        for n, c in enumerate(starts):
            g, r0 = divmod(c, tq)
            cols = slice(c, c + LANES)
            if n + FOX_SCORES_AHEAD < len(starts):
                ahead.append(scores(starts[n + FOX_SCORES_AHEAD]))
            s = ahead[n] * scale + (cq[:, cols] - ckb_ref[g, pl.ds(k0, tq), :])
            if masked:
                s = jnp.where(key_row <= query + r0, s, NEG)
            m_old = m_ref[:, cols]
            m_new = jnp.maximum(m_old, jnp.max(s, axis=0, keepdims=True))
            alpha = jnp.exp(m_old - m_new)
            p = jnp.exp(s - m_new)
            l_ref[:, cols] = alpha * l_ref[:, cols] + jnp.sum(p, axis=0, keepdims=True)
            acc_ref[:, cols] = alpha * acc_ref[:, cols] + _dot(vtb, p.astype(BF16))
            m_ref[:, cols] = m_new

    def body(j, carry):
        block(j, False)
        return carry

    lax.fori_loop(0, i, body, 0)
    block(i, True)
    ot = acc_ref[...] / l_ref[...]
    for g in range(grp):
        o_ref[:, g * hd:(g + 1) * hd] = ot[:, g * tq:(g + 1) * tq].T.astype(o_ref.dtype)


def _fox_prompt(q, k, v, c, ct, n_seq, seq, rows_out, n_kv, hd):
    tq = _pick_tile(seq, FOX_BLOCK_CAP, LANES)
    nb = seq // tq
    nq = q.shape[1]
    nh = c.shape[1]
    grp = nq // hd // n_kv
    assert hd == LANES and tq % LANES == 0 and seq % tq == 0
    return pl.pallas_call(
        functools.partial(_fox_prompt_kernel, grp=grp, hd=hd),
        grid=(n_seq, n_kv, nb),
        in_specs=[pl.BlockSpec((tq, grp * hd), lambda b, kv, i: (b * nb + i, kv)),
                  pl.BlockSpec((seq, hd), lambda b, kv, i: (b, kv)),
                  pl.BlockSpec((seq, hd), lambda b, kv, i: (b, kv)),
                  pl.BlockSpec((seq, nh), lambda b, kv, i: (b, 0)),
                  pl.BlockSpec((1, 1, grp, seq), lambda b, kv, i: (b, kv, 0, 0))],
        out_specs=pl.BlockSpec((tq, grp * hd), lambda b, kv, i: (b * nb + i, kv)),
        out_shape=jax.ShapeDtypeStruct((rows_out, nq), BF16),
        scratch_shapes=[pltpu.VMEM((seq, hd), BF16), pltpu.VMEM((hd, seq), BF16), pltpu.VMEM((grp, seq, LANES), F32),
                        pltpu.VMEM((1, grp * tq), F32), pltpu.VMEM((1, grp * tq), F32),
                        pltpu.VMEM((hd, grp * tq), F32)],
        compiler_params=_params("parallel", "parallel", "arbitrary"),
        name="fox_prompt",
    )(q, k, v, c, ct)


def _fox_sample_kernel(pt_ref, q_ref, kn_ref, vn_ref, lfn_ref, *refs, n_pg, t_new, nh, scale):
    k_refs, v_refs, lf_refs = refs[:n_pg], refs[n_pg:2 * n_pg], refs[2 * n_pg:3 * n_pg]
    o_ref, m_ref, l_ref, acc_ref, run_ref, cn_ref = refs[3 * n_pg:]
    j = pl.program_id(1)
    pg = kn_ref.shape[1]
    ncols = t_new * nh
    assert 2 * ncols == LANES and pg == LANES
    lane = lax.broadcasted_iota(jnp.int32, (pg, LANES), 1)
    row = lax.broadcasted_iota(jnp.int32, (pg, LANES), 0)
    low = lane < ncols
    lane1 = lax.broadcasted_iota(jnp.int32, (1, LANES), 1)
    diag = row == lane
    qb = q_ref[0]

    def widen(lf_a, lf_b):
        return jnp.concatenate([lf_a] * t_new + [lf_b] * t_new, axis=-1)

    def to_rows(x):
        return jnp.sum(jnp.where(diag, x, 0.0), axis=-1, keepdims=True)

    def accumulate(s_tiles, v_pairs):
        m_old = m_ref[...]
        mx = m_old
        for s in s_tiles:
            mx = jnp.maximum(mx, jnp.max(s, axis=0, keepdims=True))
        m_new = jnp.maximum(mx, pltpu.roll(mx, ncols, 1))
        alpha = jnp.exp(m_old - m_new)
        l_new = alpha * l_ref[...]
        acc = to_rows(alpha)[:ncols] * acc_ref[...]
        for s, (va, vb) in zip(s_tiles, v_pairs):
            p = jnp.exp(s - m_new)
            l_new = l_new + jnp.sum(p, axis=0, keepdims=True)
            pt = p.T.astype(BF16)
            acc = acc + _dot(pt[:ncols], va) + _dot(pt[ncols:], vb)
        m_ref[...] = m_new
        l_ref[...] = l_new
        acc_ref[...] = acc

    @pl.when(j == 0)
    def _():
        lfw = widen(lfn_ref[0], lfn_ref[0])
        cnk = _tri_sums(_tri3(pg, "lower_incl"), lfw)
        tok = (lane % ncols) // nh
        cn = jnp.sum(jnp.where(row == tok, cnk, 0.0), axis=0, keepdims=True)
        cn_ref[...] = cn
        run_ref[...] = jnp.zeros(run_ref.shape, F32)
        m_ref[...] = jnp.full(m_ref.shape, NEG, F32)
        l_ref[...] = jnp.zeros(l_ref.shape, F32)
        acc_ref[...] = jnp.zeros(acc_ref.shape, F32)
        s = _dot(kn_ref[0].astype(BF16), qb) * scale + (cn - cnk)
        s = jnp.where((row <= tok) & low, s, NEG)
        vn = vn_ref[0].astype(BF16)
        accumulate([s], [(vn, vn)])

    k3 = lax.broadcasted_iota(jnp.int32, (3 * pg, pg), 0) % pg
    later3 = jnp.where(k3 > lax.broadcasted_iota(jnp.int32, (3 * pg, pg), 1), 1.0, 0.0).astype(BF16)
    col3 = lax.broadcasted_iota(jnp.int32, (LANES, 6 * nh), 0)
    r3 = lax.broadcasted_iota(jnp.int32, (LANES, 6 * nh), 1) % (2 * nh)
    spread3 = jnp.where(r3 == (col3 // ncols) * nh + col3 % nh, 1.0, 0.0).astype(BF16)
    colt = lax.broadcasted_iota(jnp.int32, (2 * nh, LANES), 1)
    spread_t = lax.broadcasted_iota(jnp.int32, (2 * nh, LANES), 0) == (colt // ncols) * nh + colt % nh
    s_tiles, v_pairs = [], []
    run = run_ref[...]
    n_kv = qb.shape[0] // k_refs[0].shape[1]

    def page_rows(ref):
        return jnp.concatenate([ref[pl.ds(kv, pg, stride=n_kv), :] for kv in range(n_kv)], axis=-1).astype(BF16)

    lft = jnp.concatenate([ref[...] for ref in lf_refs], axis=0)
    after_all = _dot(jnp.concatenate(_split3(lft), axis=1), later3)
    page_tot = jnp.sum(lft, axis=1, keepdims=True)
    qk = [_dot(page_rows(ref), qb) for ref in k_refs]
    for a in range(0, n_pg, 2):
        sa, sb = qk[a], qk[a + 1]
        after = after_all[a * nh:(a + 2) * nh]
        after_cols = _dot(spread3, jnp.concatenate(_split3(after), axis=0)).T
        tot = jnp.sum(jnp.where(spread_t, page_tot[a * nh:(a + 2) * nh], 0.0), axis=0, keepdims=True)
        tot_sw = pltpu.roll(tot, ncols, 1)
        bias = after_cols + run + jnp.where(lane1 >= ncols, tot_sw, 0.0) + cn_ref[...]
        s_tiles.append(jnp.where(low, sa, sb) * scale + bias)
        v_pairs.append((page_rows(v_refs[a]), page_rows(v_refs[a + 1])))
        run = run + tot + tot_sw
    run_ref[...] = run
    accumulate(s_tiles, v_pairs)

    @pl.when(j == pl.num_programs(1) - 1)
    def _():
        l_tot = l_ref[...] + pltpu.roll(l_ref[...], ncols, 1)
        o_ref[0] = acc_ref[...] / to_rows(l_tot)[:ncols]


def _fox_sample(q, k_new, v_new, lf_new, cache_k, cache_v, cache_lf, layer, page_table, n_kv, hd):
    n, t, nq = q.shape
    nh = nq // hd
    grp = nh // n_kv
    nk = n_kv * hd
    pg = cache_k.shape[2]
    n_pages = page_table.shape[1]
    ncols = t * nh
    n_pg = _pick_tile(n_pages, FOX_PAGES_PER_STEP, 2)
    assert n_pages % n_pg == 0 and n_pg % 2 == 0
    qt = q.reshape(n, t, n_kv, grp, hd).transpose(0, 4, 1, 2, 3)
    own = jnp.eye(n_kv, dtype=bool)[None, :, None, None, :, None]
    qb = jnp.where(own, qt[:, None], jnp.zeros((), q.dtype)).reshape(n, nk, ncols)
    qb = jnp.concatenate([qb, qb], axis=-1)
    padrows = lambda a: jnp.pad(a, ((0, 0), (0, pg - t), (0, 0)))

    rows2d = lambda c: c.reshape(c.shape[0], c.shape[1], pg * n_kv, hd)

    def page_id(i, j, pt, p):
        return pt[i, n_pages - 1 - (j * n_pg + p)]

    def kv_page(p):
        return pl.BlockSpec((None, None, pg * n_kv, hd), lambda i, j, pt: (layer, page_id(i, j, pt, p), 0, 0))

    def lf_page(p):
        return pl.BlockSpec((None, None, nh, pg), lambda i, j, pt: (layer, page_id(i, j, pt, p), 0, 0))

    per_seq = lambda a, b: pl.BlockSpec((1, a, b), lambda i, j, pt: (i, 0, 0))
    o = pl.pallas_call(
        functools.partial(_fox_sample_kernel, n_pg=n_pg, t_new=t, nh=nh, scale=hd ** -0.5),
        grid_spec=pltpu.PrefetchScalarGridSpec(
            num_scalar_prefetch=1,
            grid=(n, n_pages // n_pg),
            in_specs=[per_seq(nk, 2 * ncols), per_seq(pg, nk), per_seq(pg, nk), per_seq(pg, nh)]
            + [kv_page(p) for p in range(n_pg)] + [kv_page(p) for p in range(n_pg)]
            + [lf_page(p) for p in range(n_pg)],
            out_specs=per_seq(ncols, nk),
            scratch_shapes=[pltpu.VMEM((1, LANES), F32), pltpu.VMEM((1, LANES), F32), pltpu.VMEM((ncols, nk), F32),
                            pltpu.VMEM((1, LANES), F32), pltpu.VMEM((1, LANES), F32)]),
        out_shape=jax.ShapeDtypeStruct((n, ncols, nk), F32),
        compiler_params=_params("parallel", "arbitrary"),
        name="fox_sample",
    )(page_table, qb, padrows(k_new), padrows(v_new), padrows(lf_new), *([rows2d(cache_k)] * n_pg),
      *([rows2d(cache_v)] * n_pg),
      *([jnp.swapaxes(cache_lf, 2, 3)] * n_pg))
    o = o.reshape(n, t, n_kv, grp, n_kv, hd)
    return jnp.stack([o[:, :, kv, :, kv, :] for kv in range(n_kv)], axis=2).reshape(n, t, nq)


def _rope_tables(pos, hd):
    half = hd // 2
    inv = ROPE_THETA ** (-2.0 * jnp.arange(half, dtype=F32) / hd)
    ang = pos.astype(F32)[:, None] * inv[None, :]
    cos = jnp.concatenate([jnp.cos(ang), jnp.cos(ang)], axis=-1)
    sin = jnp.concatenate([-jnp.sin(ang), jnp.sin(ang)], axis=-1)
    reps = LANES // hd
    return jnp.tile(cos, (1, reps)), jnp.tile(sin, (1, reps))


def kernel(x_prompt, x_sample, p_prompt, p_sample, cache_conv, cache_swa_k, cache_swa_v, cache_fox_k, cache_fox_v, cache_fox_logf, page_table, norm_mix, norm_ffn, norm_ple, norm_final, conv_w_in, conv_w_dw, conv_b_dw, conv_ln_g, conv_ln_b, conv_w_out, swa_w_qkv, swa_sinks, swa_w_o, fox_w_in, fox_b_f, fox_w_o, ffn_w_gate, ffn_w_up, ffn_w_down, ple_w_proj, ple_w_gate):
    nb, seq, d = x_prompt.shape
    ns, ts, _ = x_sample.shape
    depth = norm_mix.shape[0]
    mp, ms = nb * seq, ns * ts
    m = mp + ms
    tm = _pick_tile(m, ROW_TILE_CAP, BF16_SUBLANES)
    kw = conv_w_dw.shape[1]
    window, swa_kv, swa_hd = cache_swa_k.shape[2:]
    swa_nq, swa_nk = swa_sinks.shape[1] * swa_hd, swa_kv * swa_hd
    fox_kv, fox_hd = cache_fox_k.shape[3:]
    fox_h = fox_b_f.shape[1]
    fox_nq, fox_nk = fox_h * fox_hd, fox_kv * fox_hd
    past_len = page_table.shape[1] * cache_fox_k.shape[2]

    x = jnp.concatenate([x_prompt.reshape(mp, d), x_sample.reshape(ms, d)], axis=0)
    pos = jnp.concatenate([jnp.tile(jnp.arange(seq, dtype=jnp.int32), nb),
                           jnp.tile(past_len + jnp.arange(ts, dtype=jnp.int32), ns)])
    cos, sin = _rope_tables(pos, swa_hd)
    vec = lambda a: a.reshape(1, -1)
    bf16 = lambda w: w.astype(BF16)
    conv_w_in, conv_w_out, swa_w_qkv, swa_w_o, fox_w_in, fox_w_o = map(
        bf16, (conv_w_in, conv_w_out, swa_w_qkv, swa_w_o, fox_w_in, fox_w_o))
    ple_w_proj, ple_w_gate = bf16(ple_w_proj), bf16(ple_w_gate)
    p_p = p_prompt.reshape(depth, mp, -1)
    p_s = p_sample.reshape(depth, ms, -1)

    conv_p, conv_s = [], []
    swk_p, swv_p, swk_s, swv_s = [], [], [], []
    fk_p, fv_p, fl_p, fk_s, fv_s, fl_s = [], [], [], [], [], []
    for i in range(depth):
        kind, j = i % 3, i // 3
        g_mix = vec(norm_mix[i])
        if kind == 0:
            u = _conv_in(x, g_mix, conv_w_in, j, tm)
            dc = u.shape[1]
            conv_w = (conv_w_dw[j], vec(conv_b_dw[j]), vec(conv_ln_g[j]), vec(conv_ln_b[j]))
            o_p, c_p = _conv_core(u, jnp.zeros((nb, kw - 1, dc), F32), *conv_w, nb, seq, mp)
            o_s, c_s = _conv_core(u[mp:].reshape(ns, ts, dc), cache_conv[j], *conv_w, ns, ts, None)
            conv_p.append(c_p)
            conv_s.append(c_s)
            w_o = conv_w_out
        elif kind == 1:
            q, k, v = _swa_in(x, g_mix, swa_w_qkv, j, cos, sin, tm, swa_nq, swa_nk, swa_hd)
            o_p = _swa_prompt(q, k, v, swa_sinks[j], nb, seq, mp, swa_kv, swa_hd, window)
            o_s, ks_, vs_ = _swa_sample(q[mp:].reshape(ns, ts, swa_nq), k[mp:].reshape(ns, ts, swa_nk),
                                        v[mp:].reshape(ns, ts, swa_nk), cache_swa_k[j].reshape(ns, window, swa_nk),
                                        cache_swa_v[j].reshape(ns, window, swa_nk), swa_sinks[j], swa_kv, swa_hd,
                                        window, past_len)
            k_p, v_p = k[:mp].reshape(nb, seq, swa_kv, swa_hd), v[:mp].reshape(nb, seq, swa_kv, swa_hd)
            swk_p.append(k_p[:, seq - window:])
            swv_p.append(v_p[:, seq - window:])
            swk_s.append(ks_.reshape(ns, window, swa_kv, swa_hd))
            swv_s.append(vs_.reshape(ns, window, swa_kv, swa_hd))
            w_o = swa_w_o
        else:
            q, k, v, lf = _fox_in(x, g_mix, fox_w_in, j, vec(fox_b_f[j]), tm, fox_nq, fox_nk)
            c = _fox_cumsum(lf, nb, seq)
            ct = c.reshape(nb, seq, fox_kv, fox_h // fox_kv).transpose(0, 2, 3, 1)
            o_p = _fox_prompt(q, k, v, c, ct, nb, seq, mp, fox_kv, fox_hd)
            o_s = _fox_sample(q[mp:].reshape(ns, ts, fox_nq), k[mp:].reshape(ns, ts, fox_nk),
                              v[mp:].reshape(ns, ts, fox_nk), lf[mp:].reshape(ns, ts, fox_h),
                              cache_fox_k, cache_fox_v, cache_fox_logf, j, page_table, fox_kv, fox_hd)
            fk_p.append(k[:mp].reshape(nb, seq, fox_kv, fox_hd))
            fv_p.append(v[:mp].reshape(nb, seq, fox_kv, fox_hd))
            fl_p.append(lf[:mp].reshape(nb, seq, fox_h))
            fk_s.append(k[mp:].reshape(ns, ts, fox_kv, fox_hd))
            fv_s.append(v[mp:].reshape(ns, ts, fox_kv, fox_hd))
            fl_s.append(lf[mp:].reshape(ns, ts, fox_h))
            w_o = fox_w_o
        x = _out_proj(o_p, o_s.reshape(ms, -1).astype(BF16), w_o, j, x, tm)
        x = _ffn(x, vec(norm_ffn[i]), ffn_w_gate, ffn_w_up, ffn_w_down, i)
        x = _ple(x, p_p, p_s, vec(norm_ple[i]), ple_w_proj, ple_w_gate, i, vec(norm_final), tm,
                 final=(i == depth - 1))

    return (x[:mp].reshape(nb, seq, d), x[mp:].reshape(ns, ts, d),
            jnp.stack(conv_p), jnp.stack(conv_s),
            jnp.stack(swk_p), jnp.stack(swv_p), jnp.stack(swk_s), jnp.stack(swv_s),
            jnp.stack(fk_p), jnp.stack(fv_p), jnp.stack(fl_p),
            jnp.stack(fk_s), jnp.stack(fv_s), jnp.stack(fl_s))
```
